```python
import functools
import jax, jax.numpy as jnp
from jax import lax
import numpy as np

D_MODEL = 4096
BATCH = 1
SEQ = 8192
DEPTH = 1
DEC_BATCH = 128
DEC_SEQ = 4
PAST_LEN = 2048
PAGE_SIZE = 128

HEAD_DIM = 128
N_HEADS = D_MODEL // HEAD_DIM
SB_HEADS = N_HEADS // 2
SB_KV_HEADS = 4
NSA_HEADS = N_HEADS - SB_HEADS
NSA_KV_HEADS = 2
SB_GROUP = SB_HEADS // SB_KV_HEADS
NSA_GROUP = NSA_HEADS // NSA_KV_HEADS
MIX_WIDTH = N_HEADS * HEAD_DIM
ROPE_DIM = HEAD_DIM // 4
ROPE_THETA = 500000.0
CMP_STRIDE = 16
CMP_BLOCK = 2 * CMP_STRIDE
CMP_HIDDEN = 2 * HEAD_DIM
SLC_BLOCK = 64
SLC_TOPK = 16
N_LOCAL_BLOCKS = 2
FORCE_BONUS = 1.0e4
WINDOW = 512
Q_BLOCK = 128
D_FF = ((8 * D_MODEL + 767) // 768) * 256
RMS_EPS = 1e-6
N_ADA = 6

kernel_name = 'hymba_stickbreak_nsa_decode_step'


def projection_sizes():
    g = NSA_KV_HEADS * HEAD_DIM
    s = SB_KV_HEADS * HEAD_DIM
    return [SB_HEADS * HEAD_DIM, s, s, NSA_HEADS * HEAD_DIM, g, g, g, g, g, g, 3 * NSA_HEADS]


def split_projection(proj):
    offsets = np.cumsum(projection_sizes())[:-1].tolist()
    return jnp.split(proj, offsets, axis=-1)


def rms_norm(x, g):
    xf = x.astype(jnp.float32)
    r = lax.rsqrt(jnp.mean(xf * xf, axis=-1, keepdims=True) + RMS_EPS)
    return (xf * r).astype(x.dtype) * g


def rope_partial(x, pos):
    half = ROPE_DIM // 2
    inv = ROPE_THETA ** (-jnp.arange(half, dtype=jnp.float32) / half)
    ang = pos.astype(jnp.float32)[:, None] * inv[None, :]
    cos = jnp.cos(ang)[:, None, :]
    sin = jnp.sin(ang)[:, None, :]
    xr = x[..., :ROPE_DIM].astype(jnp.float32)
    x1, x2 = xr[..., :half], xr[..., half:]
    rot = jnp.concatenate([x1 * cos - x2 * sin, x2 * cos + x1 * sin], axis=-1)
    return jnp.concatenate([rot.astype(x.dtype), x[..., ROPE_DIM:]], axis=-1)


def masked_softmax(s, mask):
    s = jnp.where(mask, s, -jnp.inf)
    m = jnp.max(s, axis=-1, keepdims=True)
    m = jnp.where(jnp.isfinite(m), m, 0.0)
    p = jnp.exp(s - m)
    return p / jnp.maximum(jnp.sum(p, axis=-1, keepdims=True), 1e-30)


def stick_breaking_attend(q, k, v, q_pos, k_pos):
    z = jnp.einsum('bqhgd,bkhd->bhgqk', q, k, preferred_element_type=jnp.float32) * (HEAD_DIM ** -0.5)
    causal = k_pos[None, :] < q_pos[:, None]
    log_stay = jnp.where(causal, jax.nn.log_sigmoid(-z), 0.0)
    log_after = lax.cumsum(log_stay, axis=4, reverse=True) - log_stay
    w = jnp.where(causal, jnp.exp(jax.nn.log_sigmoid(z) + log_after), 0.0)
    return jnp.einsum('bhgqk,bkhd->bqhgd', w.astype(v.dtype), v)


def compress_blocks(x, pe, w1, w2):
    b, n_tok, h, d = x.shape
    n_ch = n_tok // CMP_STRIDE
    ch = x.reshape(b, n_ch, CMP_STRIDE, h, d).transpose(0, 1, 3, 2, 4).reshape(b, n_ch, h, CMP_STRIDE * d)
    half = CMP_STRIDE * d
    hidden = ch[:, :-1] @ w1[:half] + ch[:, 1:] @ w1[half:] + pe.reshape(-1) @ w1
    return jax.nn.gelu(hidden) @ w2


def cmp_attend(q, kc, vc, q_pos):
    n_cmp = kc.shape[1]
    blk_end = jnp.arange(n_cmp) * CMP_STRIDE + CMP_BLOCK - 1
    s = jnp.einsum('bqhgd,bchd->bhgqc', q, kc, preferred_element_type=jnp.float32) * (HEAD_DIM ** -0.5)
    p = masked_softmax(s, blk_end[None, :] <= q_pos[:, None])
    return p, jnp.einsum('bhgqc,bchd->bqhgd', p.astype(vc.dtype), vc)


def select_blocks(p_cmp, q_pos, n_slc):
    n_cmp = p_cmp.shape[-1]
    c0 = jnp.arange(n_cmp) * CMP_STRIDE
    s0 = jnp.arange(n_slc) * SLC_BLOCK
    overlap = ((c0[:, None] < s0[None, :] + SLC_BLOCK) & (c0[:, None] + CMP_BLOCK > s0[None, :])).astype(p_cmp.dtype)
    imp = jnp.einsum('bhgqc,cn->bhqn', p_cmp, overlap)
    blk = jnp.arange(n_slc)[None, :]
    cur = (q_pos // SLC_BLOCK)[:, None]
    valid = blk <= cur
    forced = valid & ((blk == 0) | (blk > cur - N_LOCAL_BLOCKS))
    score = jnp.where(valid, imp + jnp.where(forced, FORCE_BONUS, 0.0), -1.0)
    _, idx = lax.top_k(score, min(SLC_TOPK, n_slc))
    return idx


def selected_positions(idx):
    pos = idx[..., None] * SLC_BLOCK + jnp.arange(SLC_BLOCK)
    return pos.reshape(*idx.shape[:-1], -1)


def gather_blocks(kb, idx):
    bi = jnp.arange(kb.shape[0])[:, None, None, None]
    hi = jnp.arange(kb.shape[3])[None, :, None, None]
    g = kb[bi, idx, :, hi]
    return g.reshape(*idx.shape[:-1], -1, kb.shape[-1])


def gather_paged_blocks(pool, page_table, new_rows, idx):
    per_page = PAGE_SIZE // SLC_BLOCK
    n_past = page_table.shape[1] * per_page
    db, t, h, d = new_rows.shape
    pool_blocks = pool.reshape(-1, SLC_BLOCK, h, d)
    n_new_blk = -(-t // SLC_BLOCK)
    new_blocks = jnp.pad(new_rows, ((0, 0), (0, n_new_blk * SLC_BLOCK - t), (0, 0), (0, 0)))
    new_blocks = new_blocks.reshape(db, n_new_blk, SLC_BLOCK, h, d)
    bi = jnp.arange(db)[:, None, None, None]
    hi = jnp.arange(h)[None, :, None, None]
    jp = jnp.minimum(idx, n_past - 1)
    phys = page_table[bi, jp // per_page] * per_page + jp % per_page
    g_past = pool_blocks[phys, :, hi]
    g_new = new_blocks[bi, jnp.clip(idx - n_past, 0, n_new_blk - 1), :, hi]
    g = jnp.where((idx < n_past)[..., None, None], g_past, g_new)
    return g.reshape(*idx.shape[:-1], -1, d)


def slc_attend(q, ks, vs, kpos, q_pos):
    s = jnp.einsum('bqhgd,bhqkd->bhgqk', q, ks, preferred_element_type=jnp.float32) * (HEAD_DIM ** -0.5)
    mask = (kpos <= q_pos[:, None])[:, :, None]
    p = masked_softmax(s, mask)
    return jnp.einsum('bhgqk,bhqkd->bqhgd', p.astype(vs.dtype), vs)


def win_attend(q, kw, vw, kpos, q_pos):
    s = jnp.einsum('bqhgd,bkhd->bhgqk', q, kw, preferred_element_type=jnp.float32) * (HEAD_DIM ** -0.5)
    dist = q_pos[:, None] - kpos[None, :]
    mask = (dist >= 0) & (dist < WINDOW) & (kpos[None, :] >= 0)
    p = masked_softmax(s, mask)
    return jnp.einsum('bhgqk,bkhd->bqhgd', p.astype(vw.dtype), vw)


def gather_pages(pool, page_table):
    g = pool[page_table]
    return g.reshape(g.shape[0], -1, *pool.shape[2:])


def prepare_heads(parts, pos):
    sb_q, sb_k, sb_v, nsa_q, cmp_k, cmp_v, slc_k, slc_v, win_k, win_v, gate_logits = parts
    b, t = sb_q.shape[:2]

    def heads(a):
        return a.reshape(b, t, -1, HEAD_DIM)

    q_sb = sb_q.reshape(b, t, SB_KV_HEADS, SB_GROUP, HEAD_DIM)
    q_h = heads(nsa_q)
    q_rot = rope_partial(q_h, pos).reshape(b, t, NSA_KV_HEADS, NSA_GROUP, HEAD_DIM)
    q_nsa = q_h.reshape(b, t, NSA_KV_HEADS, NSA_GROUP, HEAD_DIM)
    gates = jax.nn.sigmoid(gate_logits).reshape(b, t, 3, NSA_KV_HEADS, NSA_GROUP, 1)
    return (q_sb, heads(sb_k), heads(sb_v), q_nsa, q_rot, heads(cmp_k), heads(cmp_v),
            rope_partial(heads(slc_k), pos), heads(slc_v), rope_partial(heads(win_k), pos), heads(win_v), gates)


def combine_nsa(gates, o_cmp, o_slc, o_win):
    return gates[:, :, 0] * o_cmp + gates[:, :, 1] * o_slc + gates[:, :, 2] * o_win


def to_query_blocks(a):
    b, t = a.shape[:2]
    return jnp.moveaxis(a.reshape(b, t // Q_BLOCK, Q_BLOCK, *a.shape[2:]), 1, 0)


def from_query_blocks(o, shape):
    return jnp.moveaxis(o, 0, 1).reshape(shape)


def sb_prompt(q, k, v):
    t = q.shape[1]
    k_pos = jnp.arange(t)

    def block(args):
        qi, i = args
        return stick_breaking_attend(qi, k, v, i * Q_BLOCK + jnp.arange(Q_BLOCK), k_pos)

    o = lax.map(block, (to_query_blocks(q), jnp.arange(t // Q_BLOCK)))
    return from_query_blocks(o, q.shape)


def nsa_prompt(q, q_rot, kc, vc, ks, vs, kw, vw):
    b, t = q.shape[:2]
    n_slc = t // SLC_BLOCK
    kb = ks.reshape(b, n_slc, SLC_BLOCK, *ks.shape[2:])
    vb = vs.reshape(b, n_slc, SLC_BLOCK, *vs.shape[2:])
    pad = ((0, 0), (WINDOW, 0), (0, 0), (0, 0))
    kw_pad = jnp.pad(kw, pad)
    vw_pad = jnp.pad(vw, pad)
    span = WINDOW + Q_BLOCK

    def block(args):
        qi, qri, i = args
        start = i * Q_BLOCK
        q_pos = start + jnp.arange(Q_BLOCK)
        p_cmp, o_cmp = cmp_attend(qi, kc, vc, q_pos)
        idx = select_blocks(p_cmp, q_pos, n_slc)
        o_slc = slc_attend(qri, gather_blocks(kb, idx), gather_blocks(vb, idx), selected_positions(idx), q_pos)
        kwi = lax.dynamic_slice_in_dim(kw_pad, start, span, axis=1)
        vwi = lax.dynamic_slice_in_dim(vw_pad, start, span, axis=1)
        o_win = win_attend(qri, kwi, vwi, start - WINDOW + jnp.arange(span), q_pos)
        return o_cmp, o_slc, o_win

    o_cmp, o_slc, o_win = lax.map(block, (to_query_blocks(q), to_query_blocks(q_rot), jnp.arange(t // Q_BLOCK)))
    return from_query_blocks(o_cmp, q.shape), from_query_blocks(o_slc, q.shape), from_query_blocks(o_win, q.shape)


def prompt_mixer(parts, w):
    t = parts[0].shape[1]
    pos = jnp.arange(t)
    q_sb, k_sb, v_sb, q_nsa, q_rot, kc_raw, vc_raw, ks, vs, kw, vw, gates = prepare_heads(parts, pos)
    o_sb = sb_prompt(q_sb, k_sb, v_sb)
    kc = compress_blocks(kc_raw, w['cmp_pe_k'], w['w_cmp_k1'], w['w_cmp_k2'])
    vc = compress_blocks(vc_raw, w['cmp_pe_v'], w['w_cmp_v1'], w['w_cmp_v2'])
    o_cmp, o_slc, o_win = nsa_prompt(q_nsa, q_rot, kc, vc, ks, vs, kw, vw)
    o_nsa = combine_nsa(gates, o_cmp, o_slc, o_win)
    win = min(WINDOW, t)
    states = (k_sb, v_sb, kc_raw, vc_raw, ks, vs, kw[:, t - win:], vw[:, t - win:])
    return o_sb, o_nsa, states


def sample_mixer(parts, w, caches, page_table):
    sb_k_pool, sb_v_pool, cmp_k_pool, cmp_v_pool, slc_k_pool, slc_v_pool, win_k_buf, win_v_buf = caches
    t = parts[0].shape[1]
    past_len = page_table.shape[1] * PAGE_SIZE
    pos = past_len + jnp.arange(t)
    q_sb, k_sb, v_sb, q_nsa, q_rot, kc_raw, vc_raw, ks, vs, kw, vw, gates = prepare_heads(parts, pos)
    k_all = jnp.concatenate([gather_pages(sb_k_pool, page_table), k_sb], axis=1)
    v_all = jnp.concatenate([gather_pages(sb_v_pool, page_table), v_sb], axis=1)
    o_sb = stick_breaking_attend(q_sb, k_all, v_all, pos, jnp.arange(past_len + t))
    n_pad = -(past_len + t) % CMP_STRIDE
    pad = ((0, 0), (0, n_pad), (0, 0), (0, 0))
    kc_seq = jnp.pad(jnp.concatenate([gather_pages(cmp_k_pool, page_table), kc_raw], axis=1), pad)
    vc_seq = jnp.pad(jnp.concatenate([gather_pages(cmp_v_pool, page_table), vc_raw], axis=1), pad)
    kc = compress_blocks(kc_seq, w['cmp_pe_k'], w['w_cmp_k1'], w['w_cmp_k2'])
    vc = compress_blocks(vc_seq, w['cmp_pe_v'], w['w_cmp_v1'], w['w_cmp_v2'])
    p_cmp, o_cmp = cmp_attend(q_nsa, kc, vc, pos)
    n_slc = past_len // SLC_BLOCK + -(-t // SLC_BLOCK)
    idx = select_blocks(p_cmp, pos, n_slc)
    o_slc = slc_attend(q_rot, gather_paged_blocks(slc_k_pool, page_table, ks, idx),
                       gather_paged_blocks(slc_v_pool, page_table, vs, idx), selected_positions(idx), pos)
    win = win_k_buf.shape[1]
    kw_all = jnp.concatenate([win_k_buf, kw], axis=1)
    vw_all = jnp.concatenate([win_v_buf, vw], axis=1)
    o_win = win_attend(q_rot, kw_all, vw_all, past_len - win + jnp.arange(win + t), pos)
    o_nsa = combine_nsa(gates, o_cmp, o_slc, o_win)
    states = (k_sb, v_sb, kc_raw, vc_raw, ks, vs, kw_all[:, t:], vw_all[:, t:])
    return o_sb, o_nsa, states


def decoder_layer(x, c, w, mix_fn):
    b, t = x.shape[:2]
    mod = jax.nn.silu(c) @ w['w_ada'] + w['b_ada']
    shift_m, scale_m, gate_m, shift_f, scale_f, gate_f = jnp.split(mod[:, None, :], N_ADA, axis=-1)
    h = rms_norm(x, w['g_pre_mix']) * (1.0 + scale_m) + shift_m
    parts = split_projection(h @ w['w_in'])
    o_sb, o_nsa, states = mix_fn(parts, w)
    o = jnp.concatenate([rms_norm(o_sb, w['g_out_sb']).reshape(b, t, -1),
                         rms_norm(o_nsa, w['g_out_nsa']).reshape(b, t, -1)], axis=-1) @ w['w_o']
    x = x + gate_m * rms_norm(o, w['g_post_mix'])
    h = rms_norm(x, w['g_pre_ffn']) * (1.0 + scale_f) + shift_f
    f = (jax.nn.silu(h @ w['w_gate']) * (h @ w['w_up'])) @ w['w_down']
    x = x + gate_f * rms_norm(f, w['g_post_ffn'])
    return x, states


def setup_inputs(seed: int = 0) -> dict:
    key = jax.random.key(seed)
    k = jax.random.split(key, 36)
    n_pages = PAST_LEN // PAGE_SIZE
    n_used = DEC_BATCH * n_pages
    n_pool = (5 * n_used + 3) // 4
    win = min(WINDOW, PAST_LEN)
    in_cols = sum(projection_sizes())

    def nrm(kk, shape, scale=1.0):
        return jax.random.normal(kk, shape, jnp.float32) * scale

    def gain(kk, n):
        return 1.0 + nrm(kk, (DEPTH, n), 0.02)

    sb_pool = (DEPTH, n_pool, PAGE_SIZE, SB_KV_HEADS, HEAD_DIM)
    nsa_pool = (DEPTH, n_pool, PAGE_SIZE, NSA_KV_HEADS, HEAD_DIM)
    win_buf = (DEPTH, DEC_BATCH, win, NSA_KV_HEADS, HEAD_DIM)
    page_table = jax.random.permutation(k[10], n_pool)[:n_used].reshape(DEC_BATCH, n_pages).astype(jnp.int32)
    cmp_in = CMP_BLOCK * HEAD_DIM
    return {
        'x_prompt': nrm(k[0], (BATCH, SEQ, D_MODEL)),
        'x_sample': nrm(k[1], (DEC_BATCH, DEC_SEQ, D_MODEL)),
        'cache_sb_k': nrm(k[2], sb_pool),
        'cache_sb_v': nrm(k[3], sb_pool),
        'cache_cmp_k': nrm(k[4], nsa_pool),
        'cache_cmp_v': nrm(k[5], nsa_pool),
        'cache_slc_k': nrm(k[6], nsa_pool),
        'cache_slc_v': nrm(k[7], nsa_pool),
        'cache_win_k': nrm(k[8], win_buf),
        'cache_win_v': nrm(k[9], win_buf),
        'page_table': page_table,
        'c_prompt': nrm(k[11], (BATCH, D_MODEL)),
        'c_sample': nrm(k[12], (DEC_BATCH, D_MODEL)),
        'w_ada': nrm(k[13], (DEPTH, D_MODEL, N_ADA * D_MODEL), 0.5 * D_MODEL ** -0.5),
        'b_ada': nrm(k[14], (DEPTH, N_ADA * D_MODEL), 0.01),
        'g_pre_mix': gain(k[15], D_MODEL),
        'w_in': nrm(k[16], (DEPTH, D_MODEL, in_cols), D_MODEL ** -0.5),
        'cmp_pe_k': nrm(k[17], (DEPTH, CMP_BLOCK, HEAD_DIM), 0.5),
        'w_cmp_k1': nrm(k[18], (DEPTH, cmp_in, CMP_HIDDEN), cmp_in ** -0.5),
        'w_cmp_k2': nrm(k[19], (DEPTH, CMP_HIDDEN, HEAD_DIM), 2.0 * CMP_HIDDEN ** -0.5),
        'cmp_pe_v': nrm(k[20], (DEPTH, CMP_BLOCK, HEAD_DIM), 0.5),
        'w_cmp_v1': nrm(k[21], (DEPTH, cmp_in, CMP_HIDDEN), cmp_in ** -0.5),
        'w_cmp_v2': nrm(k[22], (DEPTH, CMP_HIDDEN, HEAD_DIM), 2.0 * CMP_HIDDEN ** -0.5),
        'g_out_sb': gain(k[23], HEAD_DIM),
        'g_out_nsa': gain(k[24], HEAD_DIM),
        'w_o': nrm(k[25], (DEPTH, MIX_WIDTH, D_MODEL), MIX_WIDTH ** -0.5),
        'g_post_mix': gain(k[26], D_MODEL),
        'g_pre_ffn': gain(k[27], D_MODEL),
        'w_gate': nrm(k[28], (DEPTH, D_MODEL, D_FF), D_MODEL ** -0.5),
        'w_up': nrm(k[29], (DEPTH, D_MODEL, D_FF), D_MODEL ** -0.5),
        'w_down': nrm(k[30], (DEPTH, D_FF, D_MODEL), D_FF ** -0.5),
        'g_post_ffn': gain(k[31], D_MODEL),
    }


def reference(x_prompt, x_sample, cache_sb_k, cache_sb_v, cache_cmp_k, cache_cmp_v, cache_slc_k, cache_slc_v,
              cache_win_k, cache_win_v, page_table, c_prompt, c_sample, w_ada, b_ada, g_pre_mix, w_in,
              cmp_pe_k, w_cmp_k1, w_cmp_k2, cmp_pe_v, w_cmp_v1, w_cmp_v2, g_out_sb, g_out_nsa, w_o,
              g_post_mix, g_pre_ffn, w_gate, w_up, w_down, g_post_ffn):
    y_p = x_prompt
    y_s = x_sample
    per_layer = []
    for l in range(DEPTH):
        w = {'w_ada': w_ada[l], 'b_ada': b_ada[l], 'g_pre_mix': g_pre_mix[l], 'w_in': w_in[l],
             'cmp_pe_k': cmp_pe_k[l], 'w_cmp_k1': w_cmp_k1[l], 'w_cmp_k2': w_cmp_k2[l],
             'cmp_pe_v': cmp_pe_v[l], 'w_cmp_v1': w_cmp_v1[l], 'w_cmp_v2': w_cmp_v2[l],
             'g_out_sb': g_out_sb[l], 'g_out_nsa': g_out_nsa[l], 'w_o': w_o[l], 'g_post_mix': g_post_mix[l],
             'g_pre_ffn': g_pre_ffn[l], 'w_gate': w_gate[l], 'w_up': w_up[l], 'w_down': w_down[l],
             'g_post_ffn': g_post_ffn[l]}
        y_p, st_p = decoder_layer(y_p, c_prompt, w, prompt_mixer)
        caches = (cache_sb_k[l], cache_sb_v[l], cache_cmp_k[l], cache_cmp_v[l],
                  cache_slc_k[l], cache_slc_v[l], cache_win_k[l], cache_win_v[l])
        y_s, st_s = decoder_layer(y_s, c_sample, w,
                                  functools.partial(sample_mixer, caches=caches, page_table=page_table))
        per_layer.append(st_p + st_s)
    new = [jnp.stack(s) for s in zip(*per_layer)]
    return (y_p, y_s, new[0], new[1], new[2], new[3], new[4], new[5], new[6], new[7],
            new[8], new[9], new[10], new[11], new[12], new[13], new[14], new[15])
```

```python
import functools

import numpy as np
import jax
import jax.numpy as jnp
from jax import lax
from jax.experimental import pallas as pl
from jax.experimental.pallas import tpu as pltpu

F32 = jnp.float32
BF16 = jnp.bfloat16

HEAD_DIM = 128
SB_HEADS = 16
SB_KV_HEADS = 4
SB_GROUP = SB_HEADS // SB_KV_HEADS
NSA_HEADS = 16
NSA_KV_HEADS = 2
NSA_GROUP = NSA_HEADS // NSA_KV_HEADS
ROPE_DIM = HEAD_DIM // 4
ROPE_THETA = 500000.0
CMP_STRIDE = 16
CMP_BLOCK = 32
CMP_HIDDEN = 2 * HEAD_DIM
SLC_BLOCK = 64
SLC_TOPK = 16
N_LOCAL_BLOCKS = 2
FORCE_BONUS = 1.0e4
WINDOW = 512
RMS_EPS = 1e-6
N_ADA = 6
PAGE = 128
SCALE = HEAD_DIM ** -0.5
SLC_SHIFT = 6
SB_GROUP_SHIFT = 2
NSA_GROUP_SHIFT = 3

LANES = 128
MASK_BIAS = -float(2 ** 20)
NEG = -1.0e30
SB_DEAD = -110.0
VMEM_LIMIT = 56 * 1024 * 1024

OFF_SB_Q = 0
OFF_SB_K = 2048
OFF_SB_V = 2560
OFF_NSA_Q = 3072
OFF_CMP_K = 5120
OFF_CMP_V = 5376
OFF_SLC_K = 5632
OFF_SLC_V = 5888
OFF_WIN_K = 6144
OFF_WIN_V = 6400
OFF_GATE = 6656
N_GATE = 3 * NSA_HEADS


def _params(sem, vmem=VMEM_LIMIT):
    return pltpu.CompilerParams(dimension_semantics=sem, vmem_limit_bytes=vmem)


def _mm_kernel(*refs, nk, a_silu, has_bias):
    if has_bias:
        a_ref, w_ref, b_ref, o_ref, acc_ref = refs
    else:
        a_ref, w_ref, o_ref, acc_ref = refs
        b_ref = None
    a = a_ref[...]
    if a_silu:
        a = a * jax.nn.sigmoid(a)
    p = jnp.dot(a.astype(BF16), w_ref[...].astype(BF16), preferred_element_type=F32)

    def finish(v):
        if has_bias:
            v = v + b_ref[...]
        o_ref[...] = v.astype(o_ref.dtype)

    if nk == 1:
        finish(p)
    else:
        k = pl.program_id(2)

        @pl.when(k == 0)
        def _():
            acc_ref[...] = p

        @pl.when(k > 0)
        def _():
            acc_ref[...] += p

        @pl.when(k == nk - 1)
        def _():
            finish(acc_ref[...])


def matmul(a, w, *, tm, tn, tk=None, n_cols=None, bias=None, a_silu=False, out_dtype=F32, name="matmul"):
    m, kdim = a.shape
    n = w.shape[1] if n_cols is None else n_cols
    tk = kdim if tk is None else tk
    tm = min(tm, m)
    assert m % tm == 0 and n % tn == 0 and kdim % tk == 0
    nk = kdim // tk
    in_specs = [pl.BlockSpec((tm, tk), lambda i, j, k: (i, k)),
                pl.BlockSpec((tk, tn), lambda i, j, k: (k, j))]
    args = [a, w]
    if bias is not None:
        in_specs.append(pl.BlockSpec((1, tn), lambda i, j, k: (0, j)))
        args.append(bias.reshape(1, -1))
    acc_shape = (tm, tn) if nk > 1 else (8, LANES)
    return pl.pallas_call(
        functools.partial(_mm_kernel, nk=nk, a_silu=a_silu, has_bias=bias is not None),
        grid=(m // tm, n // tn, nk),
        in_specs=in_specs,
        out_specs=pl.BlockSpec((tm, tn), lambda i, j, k: (i, j)),
        out_shape=jax.ShapeDtypeStruct((m, n), out_dtype),
        scratch_shapes=[pltpu.VMEM(acc_shape, F32)],
        compiler_params=_params(("parallel", "parallel", "arbitrary")),
        name=name,
    )(*args)


def _gate_up_kernel(a_ref, wg_ref, wu_ref, o_ref):
    a = a_ref[...]
    g = jnp.dot(a, wg_ref[...].astype(BF16), preferred_element_type=F32)
    u = jnp.dot(a, wu_ref[...].astype(BF16), preferred_element_type=F32)
    o_ref[...] = (g * jax.nn.sigmoid(g) * u).astype(o_ref.dtype)


def gate_up(a, wg, wu, *, tm, tn):
    m, kdim = a.shape
    n = wg.shape[1]
    tm = min(tm, m)
    assert m % tm == 0 and n % tn == 0
    return pl.pallas_call(
        _gate_up_kernel,
        grid=(m // tm, n // tn),
        in_specs=[pl.BlockSpec((tm, kdim), lambda i, j: (i, 0)),
                  pl.BlockSpec((kdim, tn), lambda i, j: (0, j)),
                  pl.BlockSpec((kdim, tn), lambda i, j: (0, j))],
        out_specs=pl.BlockSpec((tm, tn), lambda i, j: (i, j)),
        out_shape=jax.ShapeDtypeStruct((m, n), BF16),
        compiler_params=_params(("parallel", "parallel")),
        name="ffn_gate_up",
    )(a, wg, wu)


def _rms(x):
    return x * lax.rsqrt(jnp.mean(x * x, axis=-1, keepdims=True) + RMS_EPS)


def _norm_mod_kernel(x_ref, g_ref, sc_ref, sh_ref, h_ref):
    h = _rms(x_ref[...]) * g_ref[...] * (1.0 + sc_ref[...]) + sh_ref[...]
    h_ref[...] = h.astype(h_ref.dtype)


def _mod_spec(mod, tm, d):
    if mod.shape[0] == 1:
        return pl.BlockSpec((1, d), lambda i: (0, 0))
    return pl.BlockSpec((tm, d), lambda i: (i, 0))


def norm_mod(x, g, scale, shift, *, tm=256):
    m, d = x.shape
    tm = min(tm, m)
    return pl.pallas_call(
        _norm_mod_kernel,
        grid=(m // tm,),
        in_specs=[pl.BlockSpec((tm, d), lambda i: (i, 0)),
                  pl.BlockSpec((1, d), lambda i: (0, 0)),
                  _mod_spec(scale, tm, d), _mod_spec(shift, tm, d)],
        out_specs=pl.BlockSpec((tm, d), lambda i: (i, 0)),
        out_shape=jax.ShapeDtypeStruct((m, d), BF16),
        compiler_params=_params(("parallel",)),
        name="norm_mod",
    )(x, g.reshape(1, d), scale, shift)


def _resid_kernel(*refs, with_mod):
    if with_mod:
        x_ref, o_ref, gp_ref, gate_ref, g2_ref, sc_ref, sh_ref, y_ref, h_ref = refs
    else:
        x_ref, o_ref, gp_ref, gate_ref, y_ref = refs
    y = x_ref[...] + gate_ref[...] * (_rms(o_ref[...]) * gp_ref[...])
    y_ref[...] = y
    if with_mod:
        h = _rms(y) * g2_ref[...] * (1.0 + sc_ref[...]) + sh_ref[...]
        h_ref[...] = h.astype(h_ref.dtype)


def resid_norm(x, o, g_post, gate, mod2=None, *, tm=256):
    m, d = x.shape
    if gate.shape[0] != 1:
        tm = tm // 2
    tm = min(tm, m)
    row = pl.BlockSpec((tm, d), lambda i: (i, 0))
    vec = pl.BlockSpec((1, d), lambda i: (0, 0))
    in_specs = [row, row, vec, _mod_spec(gate, tm, d)]
    args = [x, o, g_post.reshape(1, d), gate]
    out_specs = [row]
    out_shape = [jax.ShapeDtypeStruct((m, d), F32)]
    if mod2 is not None:
        g2, sc, sh = mod2
        in_specs += [vec, _mod_spec(sc, tm, d), _mod_spec(sh, tm, d)]
        args += [g2.reshape(1, d), sc, sh]
        out_specs.append(row)
        out_shape.append(jax.ShapeDtypeStruct((m, d), BF16))
    return pl.pallas_call(
        functools.partial(_resid_kernel, with_mod=mod2 is not None),
        grid=(m // tm,),
        in_specs=in_specs, out_specs=out_specs, out_shape=out_shape,
        compiler_params=_params(("parallel",)),
        name="resid_norm",
    )(*args)


def _rope(x, c, sa, sb):
    half = ROPE_DIM // 2
    return x * c + pltpu.roll(x, half, 1) * sa + pltpu.roll(x, HEAD_DIM - half, 1) * sb


def _prep_kernel(proj_ref, lg_ref, c_ref, sa_ref, sb_ref,
                 qsb_ref, qn_ref, qr_ref,
                 sbk_ref, sbv_ref, sbk16_ref, sbv16_ref,
                 ck_ref, cv_ref,
                 sk_ref, sv_ref, sk16_ref, sv16_ref,
                 wk_ref, wv_ref, wk16_ref, wv16_ref,
                 gates_ref):
    c, sa, sb = c_ref[...], sa_ref[...], sb_ref[...]

    def head(off, h):
        return proj_ref[:, off + h * HEAD_DIM: off + (h + 1) * HEAD_DIM]

    for h in range(SB_HEADS):
        qsb_ref[h] = head(OFF_SB_Q, h).astype(BF16)
    for h in range(NSA_HEADS):
        q = head(OFF_NSA_Q, h)
        qn_ref[h] = q.astype(BF16)
        qr_ref[h] = _rope(q, c, sa, sb).astype(BF16)
    k = proj_ref[:, OFF_SB_K:OFF_SB_K + 512]
    v = proj_ref[:, OFF_SB_V:OFF_SB_V + 512]
    sbk_ref[...] = k
    sbv_ref[...] = v
    sbk16_ref[...] = k.astype(BF16)
    sbv16_ref[...] = v.astype(BF16)
    ck_ref[...] = proj_ref[:, OFF_CMP_K:OFF_CMP_K + 256]
    cv_ref[...] = proj_ref[:, OFF_CMP_V:OFF_CMP_V + 256]
    for h in range(NSA_KV_HEADS):
        sl = slice(h * HEAD_DIM, (h + 1) * HEAD_DIM)
        ks = _rope(head(OFF_SLC_K, h), c, sa, sb)
        sk_ref[:, sl] = ks
        sk16_ref[:, sl] = ks.astype(BF16)
        kw = _rope(head(OFF_WIN_K, h), c, sa, sb)
        wk_ref[:, sl] = kw
        wk16_ref[:, sl] = kw.astype(BF16)
    vs = proj_ref[:, OFF_SLC_V:OFF_SLC_V + 256]
    sv_ref[...] = vs
    sv16_ref[...] = vs.astype(BF16)
    vw = proj_ref[:, OFF_WIN_V:OFF_WIN_V + 256]
    wv_ref[...] = vw
    wv16_ref[...] = vw.astype(BF16)
    gates_ref[...] = jax.nn.sigmoid(lg_ref[...])


def rope_tables(pos):
    half = ROPE_DIM // 2
    inv = ROPE_THETA ** (-jnp.arange(half, dtype=F32) / half)
    ang = pos.astype(F32)[:, None] * inv[None, :]
    cos, sin = jnp.cos(ang), jnp.sin(ang)
    n = pos.shape[0]
    rest = HEAD_DIM - ROPE_DIM
    c = jnp.concatenate([cos, cos, jnp.ones((n, rest), F32)], axis=1)
    sa = jnp.concatenate([jnp.zeros((n, half), F32), sin, jnp.zeros((n, rest), F32)], axis=1)
    sb = jnp.concatenate([-sin, jnp.zeros((n, half + rest), F32)], axis=1)
    return c, sa, sb


def prep_heads(proj, logits, pos, *, tm=256):
    m = proj.shape[0]
    tm = min(tm, m)
    c, sa, sb = rope_tables(pos)
    row = lambda w: pl.BlockSpec((tm, w), lambda i: (i, 0))
    hm = pl.BlockSpec((16, tm, HEAD_DIM), lambda i: (0, i, 0))
    f = lambda w, dt: jax.ShapeDtypeStruct((m, w), dt)
    hshape = jax.ShapeDtypeStruct((16, m, HEAD_DIM), BF16)
    out_specs = [hm, hm, hm,
                 row(512), row(512), row(512), row(512),
                 row(256), row(256),
                 row(256), row(256), row(256), row(256),
                 row(256), row(256), row(256), row(256),
                 row(LANES)]
    out_shape = [hshape, hshape, hshape,
                 f(512, F32), f(512, F32), f(512, BF16), f(512, BF16),
                 f(256, F32), f(256, F32),
                 f(256, F32), f(256, F32), f(256, BF16), f(256, BF16),
                 f(256, F32), f(256, F32), f(256, BF16), f(256, BF16),
                 f(LANES, F32)]
    return pl.pallas_call(
        _prep_kernel,
        grid=(m // tm,),
        in_specs=[row(proj.shape[1]), row(LANES), row(LANES), row(LANES), row(LANES)],
        out_specs=out_specs, out_shape=out_shape,
        compiler_params=_params(("parallel",)),
        name="prep_heads",
    )(proj, logits, c, sa, sb)


def _nt(a, b):
    return lax.dot_general(a, b, (((1,), (1,)), ((), ())), preferred_element_type=F32)


def _cumsum_matrix():
    j = lax.broadcasted_iota(jnp.int32, (2 * LANES, 2 * LANES), 0) & (LANES - 1)
    s = lax.broadcasted_iota(jnp.int32, (2 * LANES, 2 * LANES), 1)
    return jnp.where((s >= LANES) | (j > s), 1.0, 0.0).astype(BF16)


def _split_hi_lo(x):
    hi = x.astype(BF16)
    lo = (x - hi.astype(F32)).astype(BF16)
    return jnp.concatenate([hi, lo], axis=1)


def _sb_block(z, causal, carry, um):
    sp = jnp.maximum(z, 0.0) + jnp.log1p(jnp.exp(-jnp.abs(z)))
    ls = jnp.where(causal, -sp, 0.0)
    cb = jnp.dot(_split_hi_lo(ls), um, preferred_element_type=F32)
    w = jnp.where(causal, jnp.exp(z + ls + cb[:, :LANES] + carry), 0.0)
    return w, carry + cb[:, LANES:]


def _softmax_rows(s, mask):
    s = jnp.where(mask, s, NEG)
    m = jnp.max(s, axis=-1, keepdims=True)
    p = jnp.where(mask, jnp.exp(s - m), 0.0)
    return p / jnp.maximum(jnp.sum(p, axis=-1, keepdims=True), 1e-30)


def _sb_prompt_kernel(q_ref, k_ref, v_ref, o_ref, acc_ref, carry_ref, *, tq):
    qi = pl.program_id(1)
    rows = SB_GROUP * tq
    q = q_ref[...].reshape(rows, HEAD_DIM)
    um = _cumsum_matrix()
    qpos = qi * tq + (lax.broadcasted_iota(jnp.int32, (rows, LANES), 0) & (tq - 1))
    lane = lax.broadcasted_iota(jnp.int32, (rows, LANES), 1)
    acc_ref[...] = jnp.zeros_like(acc_ref)
    carry_ref[...] = jnp.zeros_like(carry_ref)

    def body(state):
        kb, _ = state
        start = pl.multiple_of(kb * LANES, LANES)
        kt = k_ref[pl.ds(start, LANES), :]
        vt = v_ref[pl.ds(start, LANES), :]
        z = _nt(q, kt) * SCALE
        causal = (start + lane) < qpos
        w, carry = _sb_block(z, causal, carry_ref[...], um)
        acc_ref[...] += jnp.dot(w.astype(BF16), vt, preferred_element_type=F32)
        carry_ref[...] = carry
        return kb - 1, jnp.max(carry)

    lax.while_loop(lambda st: (st[0] >= 0) & (st[1] > SB_DEAD), body,
                   ((qi * tq + tq) // LANES - 1, jnp.float32(0.0)))
    o_ref[...] = acc_ref[...].reshape(1, SB_GROUP, tq, HEAD_DIM)


def sb_prompt(q, k16, v16, *, tq=128):
    t = q.shape[1]
    qv = q.reshape(SB_KV_HEADS, SB_GROUP, t, HEAD_DIM)
    out = pl.pallas_call(
        functools.partial(_sb_prompt_kernel, tq=tq),
        grid=(SB_KV_HEADS, t // tq),
        in_specs=[pl.BlockSpec((1, SB_GROUP, tq, HEAD_DIM), lambda h, i: (h, 0, i, 0)),
                  pl.BlockSpec((t, HEAD_DIM), lambda h, i: (0, h)),
                  pl.BlockSpec((t, HEAD_DIM), lambda h, i: (0, h))],
        out_specs=pl.BlockSpec((1, SB_GROUP, tq, HEAD_DIM), lambda h, i: (h, 0, i, 0)),
        out_shape=jax.ShapeDtypeStruct((SB_KV_HEADS, SB_GROUP, t, HEAD_DIM), F32),
        scratch_shapes=[pltpu.VMEM((SB_GROUP * tq, HEAD_DIM), F32),
                        pltpu.VMEM((SB_GROUP * tq, LANES), F32)],
        compiler_params=_params(("parallel", "arbitrary")),
        name="sb_prompt",
    )(qv, k16, v16)
    return out.reshape(SB_HEADS, t, HEAD_DIM)


def _compress(get_chunk_rows, w1_ref, b_ref, w2_ref, n_chunks):
    ab = jnp.zeros((n_chunks, 2 * CMP_HIDDEN), F32)
    for j in range(CMP_STRIDE):
        ab = ab + jnp.dot(get_chunk_rows(j).astype(BF16), w1_ref[j], preferred_element_type=F32)
    first = ab[:, :CMP_HIDDEN]
    second = ab[:, CMP_HIDDEN:]
    nxt = jnp.concatenate([second[1:], second[:1]], axis=0)
    hidden = first + nxt + b_ref[...]
    return jnp.dot(jax.nn.gelu(hidden).astype(BF16), w2_ref[...], preferred_element_type=F32)


def _cmp_weights(w1, w2):
    half = CMP_STRIDE * HEAD_DIM
    a = w1[:half].reshape(CMP_STRIDE, HEAD_DIM, CMP_HIDDEN)
    b = w1[half:].reshape(CMP_STRIDE, HEAD_DIM, CMP_HIDDEN)
    return jnp.concatenate([a, b], axis=-1).astype(BF16), w2.astype(BF16)


def _pe_bias(pe, w1):
    row = jnp.zeros((8, pe.size), F32).at[0].set(pe.reshape(-1))
    return matmul(row, w1, tm=8, tn=CMP_HIDDEN, name="cmp_pe_bias")[:1]


def _compress_prompt_kernel(x_ref, w1_ref, b_ref, w2_ref, o_ref, *, n_chunks):
    get = lambda j: x_ref[pl.ds(j, n_chunks, stride=CMP_STRIDE), :]
    o_ref[0, 0] = _compress(get, w1_ref.at[0], b_ref.at[0], w2_ref.at[0], n_chunks).astype(o_ref.dtype)


def compress_prompt(ck, cv, w1s, bs, w2s):
    t = ck.shape[0]
    n_chunks = t // CMP_STRIDE
    x = jnp.stack([ck, cv])
    return pl.pallas_call(
        functools.partial(_compress_prompt_kernel, n_chunks=n_chunks),
        grid=(2, NSA_KV_HEADS),
        in_specs=[pl.BlockSpec((None, t, HEAD_DIM), lambda a, h: (a, 0, h)),
                  pl.BlockSpec((1, CMP_STRIDE, HEAD_DIM, 2 * CMP_HIDDEN), lambda a, h: (a, 0, 0, 0)),
                  pl.BlockSpec((1, 1, CMP_HIDDEN), lambda a, h: (a, 0, 0)),
                  pl.BlockSpec((1, CMP_HIDDEN, HEAD_DIM), lambda a, h: (a, 0, 0))],
        out_specs=pl.BlockSpec((1, 1, n_chunks, HEAD_DIM), lambda a, h: (a, h, 0, 0)),
        out_shape=jax.ShapeDtypeStruct((2, NSA_KV_HEADS, n_chunks, HEAD_DIM), BF16),
        compiler_params=_params(("parallel", "parallel")),
        name="compress_prompt",
    )(x, w1s, bs, w2s)


def _overlap_t(n_cmp_pad):
    n = lax.broadcasted_iota(jnp.int32, (LANES, n_cmp_pad), 0)
    c0 = lax.broadcasted_iota(jnp.int32, (LANES, n_cmp_pad), 1) * CMP_STRIDE
    s0 = n * SLC_BLOCK
    return jnp.where((c0 < s0 + SLC_BLOCK) & (c0 + CMP_BLOCK > s0), 1.0, 0.0).astype(BF16)


def _select_bias_t(imp_t, qpos_row, score_ref):
    shape = imp_t.shape
    blk = lax.broadcasted_iota(jnp.int32, shape, 0)
    cur = qpos_row >> SLC_SHIFT
    valid = blk <= cur
    forced = valid & ((blk == 0) | (blk > cur - N_LOCAL_BLOCKS))
    score = jnp.where(valid, imp_t + jnp.where(forced, FORCE_BONUS, 0.0), -1.0)
    score_ref[...] = score

    def body(m, rank):
        sm = score_ref[pl.ds(m, 1), :]
        return rank + jnp.where(blk > m, jnp.where(sm >= score, 1.0, 0.0), jnp.where(sm > score, 1.0, 0.0))

    rank = lax.fori_loop(0, shape[0], body, jnp.zeros(shape, F32))
    return jnp.where(valid & (rank < SLC_TOPK), 0.0, MASK_BIAS)


def _cmp_prompt_kernel(q_ref, kc_ref, vc_ref, o_ref, bias_ref, score_ref, *, tq, n_cmp):
    qi = pl.program_id(1)
    rows = NSA_GROUP * tq
    q = q_ref[...].reshape(rows, HEAD_DIM)
    kc = kc_ref[0, 0]
    vc = vc_ref[0, 0]
    n_pad = kc.shape[0]
    s = _nt(q, kc) * SCALE
    qpos = qi * tq + (lax.broadcasted_iota(jnp.int32, (rows, n_pad), 0) & (tq - 1))
    c = lax.broadcasted_iota(jnp.int32, (rows, n_pad), 1)
    mask = (c * CMP_STRIDE + (CMP_BLOCK - 1) <= qpos) & (c < n_cmp)
    p = _softmax_rows(s, mask)
    o = jnp.dot(p.astype(BF16), vc, preferred_element_type=F32)
    o_ref[...] = o.reshape(1, NSA_GROUP, tq, HEAD_DIM)
    psum = p[:tq]
    for g in range(1, NSA_GROUP):
        psum = psum + p[g * tq:(g + 1) * tq]
    hi = psum.astype(BF16)
    lo = (psum - hi.astype(F32)).astype(BF16)
    ov = _overlap_t(n_pad)
    imp_t = _nt(ov, hi) + _nt(ov, lo)
    qpos_row = qi * tq + lax.broadcasted_iota(jnp.int32, (1, tq), 1)
    bias_ref[0] = _select_bias_t(imp_t, qpos_row, score_ref)


def cmp_prompt(q, kvc, *, tq=128):
    t = q.shape[1]
    n_chunks = kvc.shape[2]
    qv = q.reshape(NSA_KV_HEADS, NSA_GROUP, t, HEAD_DIM)
    o, bias_t = pl.pallas_call(
        functools.partial(_cmp_prompt_kernel, tq=tq, n_cmp=n_chunks - 1),
        grid=(NSA_KV_HEADS, t // tq),
        in_specs=[pl.BlockSpec((1, NSA_GROUP, tq, HEAD_DIM), lambda h, i: (h, 0, i, 0)),
                  pl.BlockSpec((1, 1, n_chunks, HEAD_DIM), lambda h, i: (0, h, 0, 0)),
                  pl.BlockSpec((1, 1, n_chunks, HEAD_DIM), lambda h, i: (1, h, 0, 0))],
        out_specs=[pl.BlockSpec((1, NSA_GROUP, tq, HEAD_DIM), lambda h, i: (h, 0, i, 0)),
                   pl.BlockSpec((1, LANES, tq), lambda h, i: (h, 0, i))],
        out_shape=[jax.ShapeDtypeStruct((NSA_KV_HEADS, NSA_GROUP, t, HEAD_DIM), F32),
                   jax.ShapeDtypeStruct((NSA_KV_HEADS, LANES, t), F32)],
        scratch_shapes=[pltpu.VMEM((LANES, tq), F32)],
        compiler_params=_params(("parallel", "parallel")),
        name="cmp_prompt",
    )(qv, kvc, kvc)
    return o.reshape(NSA_HEADS, t, HEAD_DIM), bias_t


def _flash_step(s, mask, v, m_ref, l_ref, acc_ref):
    s = jnp.where(mask, s, NEG)
    m_old = m_ref[...]
    m_new = jnp.maximum(m_old, jnp.max(s, axis=-1, keepdims=True))
    alpha = jnp.exp(m_old - m_new)
    p = jnp.where(mask, jnp.exp(s - m_new), 0.0)
    l_ref[...] = alpha * l_ref[...] + jnp.sum(p, axis=-1, keepdims=True)
    acc_ref[...] = alpha * acc_ref[...] + jnp.dot(p.astype(BF16), v, preferred_element_type=F32)
    m_ref[...] = m_new


def _flash_init(m_ref, l_ref, acc_ref):
    m_ref[...] = jnp.full_like(m_ref, NEG)
    l_ref[...] = jnp.zeros_like(l_ref)
    acc_ref[...] = jnp.zeros_like(acc_ref)


def _slc_prompt_kernel(q_ref, bias_ref, k_ref, v_ref, o_ref, m_ref, l_ref, acc_ref, *, tq, tk):
    qi = pl.program_id(1)
    rows = NSA_GROUP * tq
    q = q_ref[...].reshape(rows, HEAD_DIM)
    bias = bias_ref[0]
    qa = jnp.concatenate([q, jnp.concatenate([bias] * NSA_GROUP, axis=0)], axis=1)
    qpos = qi * tq + (lax.broadcasted_iota(jnp.int32, (rows, tk), 0) & (tq - 1))
    lane = lax.broadcasted_iota(jnp.int32, (rows, tk), 1)
    blk_row = lax.broadcasted_iota(jnp.int32, (tk, LANES), 0) >> SLC_SHIFT
    blk_col = lax.broadcasted_iota(jnp.int32, (tk, LANES), 1)
    _flash_init(m_ref, l_ref, acc_ref)

    def body(i, carry):
        kb = (qi * tq) // tk - i
        start = pl.multiple_of(kb * tk, tk)
        onehot = jnp.where(blk_row + kb * (tk // SLC_BLOCK) == blk_col, 1.0, 0.0).astype(BF16)
        ka = jnp.concatenate([k_ref[pl.ds(start, tk), :], onehot], axis=1)
        s = _nt(qa, ka) * SCALE
        mask = ((start + lane) <= qpos) & (s > 0.5 * MASK_BIAS * SCALE)
        _flash_step(s, mask, v_ref[pl.ds(start, tk), :], m_ref, l_ref, acc_ref)
        return carry

    lax.fori_loop(0, (qi * tq) // tk + 1, body, 0)
    o = acc_ref[...] / jnp.maximum(l_ref[...], 1e-30)
    o_ref[...] = o.reshape(1, NSA_GROUP, tq, HEAD_DIM)


def slc_prompt(q, bias, k16, v16, *, tq=128, tk=128):
    t = q.shape[1]
    qv = q.reshape(NSA_KV_HEADS, NSA_GROUP, t, HEAD_DIM)
    rows = NSA_GROUP * tq
    out = pl.pallas_call(
        functools.partial(_slc_prompt_kernel, tq=tq, tk=tk),
        grid=(NSA_KV_HEADS, t // tq),
        in_specs=[pl.BlockSpec((1, NSA_GROUP, tq, HEAD_DIM), lambda h, i: (h, 0, i, 0)),
                  pl.BlockSpec((1, tq, LANES), lambda h, i: (h, i, 0)),
                  pl.BlockSpec((t, HEAD_DIM), lambda h, i: (0, h)),
                  pl.BlockSpec((t, HEAD_DIM), lambda h, i: (0, h))],
        out_specs=pl.BlockSpec((1, NSA_GROUP, tq, HEAD_DIM), lambda h, i: (h, 0, i, 0)),
        out_shape=jax.ShapeDtypeStruct((NSA_KV_HEADS, NSA_GROUP, t, HEAD_DIM), F32),
        scratch_shapes=[pltpu.VMEM((rows, 1), F32), pltpu.VMEM((rows, 1), F32),
                        pltpu.VMEM((rows, HEAD_DIM), F32)],
        compiler_params=_params(("parallel", "arbitrary")),
        name="slc_prompt",
    )(qv, bias, k16, v16)
    return out.reshape(NSA_HEADS, t, HEAD_DIM)


def _win_prompt_kernel(q_ref, k_ref, v_ref, o_ref, m_ref, l_ref, acc_ref, *, tq):
    qi = pl.program_id(1)
    rows = NSA_GROUP * tq
    q = q_ref[...].reshape(rows, HEAD_DIM)
    qpos = qi * tq + (lax.broadcasted_iota(jnp.int32, (rows, tq), 0) & (tq - 1))
    lane = lax.broadcasted_iota(jnp.int32, (rows, tq), 1)
    _flash_init(m_ref, l_ref, acc_ref)

    def body(i, carry):
        start = pl.multiple_of((qi - i) * tq, tq)
        dist = qpos - (start + lane)
        mask = (dist >= 0) & (dist < WINDOW)
        s = _nt(q, k_ref[pl.ds(start, tq), :]) * SCALE
        _flash_step(s, mask, v_ref[pl.ds(start, tq), :], m_ref, l_ref, acc_ref)
        return carry

    lax.fori_loop(0, jnp.minimum(qi, WINDOW // tq) + 1, body, 0)
    o = acc_ref[...] / jnp.maximum(l_ref[...], 1e-30)
    o_ref[...] = o.reshape(1, NSA_GROUP, tq, HEAD_DIM)


def win_prompt(q, k16, v16, *, tq=128):
    t = q.shape[1]
    qv = q.reshape(NSA_KV_HEADS, NSA_GROUP, t, HEAD_DIM)
    rows = NSA_GROUP * tq
    out = pl.pallas_call(
        functools.partial(_win_prompt_kernel, tq=tq),
        grid=(NSA_KV_HEADS, t // tq),
        in_specs=[pl.BlockSpec((1, NSA_GROUP, tq, HEAD_DIM), lambda h, i: (h, 0, i, 0)),
                  pl.BlockSpec((t, HEAD_DIM), lambda h, i: (0, h)),
                  pl.BlockSpec((t, HEAD_DIM), lambda h, i: (0, h))],
        out_specs=pl.BlockSpec((1, NSA_GROUP, tq, HEAD_DIM), lambda h, i: (h, 0, i, 0)),
        out_shape=jax.ShapeDtypeStruct((NSA_KV_HEADS, NSA_GROUP, t, HEAD_DIM), F32),
        scratch_shapes=[pltpu.VMEM((rows, 1), F32), pltpu.VMEM((rows, 1), F32),
                        pltpu.VMEM((rows, HEAD_DIM), F32)],
        compiler_params=_params(("parallel", "arbitrary")),
        name="win_prompt",
    )(qv, k16, v16)
    return out.reshape(NSA_HEADS, t, HEAD_DIM)


def _combine_kernel(osb_ref, oc_ref, os_ref, ow_ref, gates_ref, gsb_ref, gnsa_ref, out_ref):
    gsb = gsb_ref[...]
    gnsa = gnsa_ref[...]
    gates = gates_ref[...]
    for h in range(SB_HEADS):
        out_ref[:, h * HEAD_DIM:(h + 1) * HEAD_DIM] = (_rms(osb_ref[h]) * gsb).astype(out_ref.dtype)
    for h in range(NSA_HEADS):
        o = (gates[:, h:h + 1] * oc_ref[h]
             + gates[:, NSA_HEADS + h:NSA_HEADS + h + 1] * os_ref[h]
             + gates[:, 2 * NSA_HEADS + h:2 * NSA_HEADS + h + 1] * ow_ref[h])
        col = (SB_HEADS + h) * HEAD_DIM
        out_ref[:, col:col + HEAD_DIM] = (_rms(o) * gnsa).astype(out_ref.dtype)


def combine_heads(o_sb, o_cmp, o_slc, o_win, gates, g_sb, g_nsa, *, tm=256):
    m = o_sb.shape[1]
    tm = min(tm, m)
    hm = pl.BlockSpec((16, tm, HEAD_DIM), lambda i: (0, i, 0))
    vec = pl.BlockSpec((1, HEAD_DIM), lambda i: (0, 0))
    width = (SB_HEADS + NSA_HEADS) * HEAD_DIM
    return pl.pallas_call(
        _combine_kernel,
        grid=(m // tm,),
        in_specs=[hm, hm, hm, hm, pl.BlockSpec((tm, LANES), lambda i: (i, 0)), vec, vec],
        out_specs=pl.BlockSpec((tm, width), lambda i: (i, 0)),
        out_shape=jax.ShapeDtypeStruct((m, width), BF16),
        compiler_params=_params(("parallel",)),
        name="combine_heads",
    )(o_sb, o_cmp, o_slc, o_win, gates, g_sb.reshape(1, -1), g_nsa.reshape(1, -1))


def _page_specs(n_pages, width):
    return [pl.BlockSpec((None, PAGE, width), functools.partial(lambda p, s, pt: (pt[s, p], 0, 0), p))
            for p in range(n_pages)]


def _page_head_specs(n_pages, n_heads):
    return [pl.BlockSpec((None, PAGE, HEAD_DIM), functools.partial(lambda p, h, s, pt: (pt[s, p], 0, h), p, h))
            for h in range(n_heads) for p in range(n_pages)]


def _gather_head(page_refs, new_ref, h):
    sl = slice(h * HEAD_DIM, (h + 1) * HEAD_DIM)
    parts = [r[:, sl].astype(BF16) for r in page_refs]
    parts.append(new_ref[0, :, sl])
    return jnp.concatenate(parts, axis=0)


def _sb_sample_kernel(pt_ref, q_ref, kn_ref, vn_ref, *refs, n_pages, n_new):
    k_pages = refs[:n_pages]
    v_pages = refs[n_pages:2 * n_pages]
    o_ref = refs[2 * n_pages]
    past = n_pages * PAGE
    nb = n_pages + 1
    rows = n_new * SB_GROUP
    um = _cumsum_matrix()
    kpos = lax.broadcasted_iota(jnp.int32, (rows, nb * PAGE), 1)
    qpos = past + (lax.broadcasted_iota(jnp.int32, (rows, nb * PAGE), 0) >> SB_GROUP_SHIFT)
    causal = kpos < qpos
    for h in range(SB_KV_HEADS):
        kh = _gather_head(k_pages, kn_ref, h)
        vh = _gather_head(v_pages, vn_ref, h)
        z = _nt(q_ref[0, h], kh) * SCALE
        carry = jnp.zeros((rows, LANES), F32)
        ws = [None] * nb
        for b in range(nb - 1, -1, -1):
            sl = slice(b * PAGE, (b + 1) * PAGE)
            ws[b], carry = _sb_block(z[:, sl], causal[:, sl], carry, um)
        w = jnp.concatenate(ws, axis=1)
        o_ref[0, h] = jnp.dot(w.astype(BF16), vh, preferred_element_type=F32)


def sb_sample(page_table, q, k_new, v_new, k_pool, v_pool):
    b, n_pages = page_table.shape
    rows = q.shape[2]
    n_new = rows // SB_GROUP
    width = SB_KV_HEADS * HEAD_DIM
    grid_spec = pltpu.PrefetchScalarGridSpec(
        num_scalar_prefetch=1, grid=(b,),
        in_specs=[pl.BlockSpec((1, SB_KV_HEADS, rows, HEAD_DIM), lambda s, pt: (s, 0, 0, 0)),
                  pl.BlockSpec((1, PAGE, width), lambda s, pt: (s, 0, 0)),
                  pl.BlockSpec((1, PAGE, width), lambda s, pt: (s, 0, 0))]
                 + _page_specs(n_pages, width) + _page_specs(n_pages, width),
        out_specs=pl.BlockSpec((1, SB_KV_HEADS, rows, HEAD_DIM), lambda s, pt: (s, 0, 0, 0)))
    return pl.pallas_call(
        functools.partial(_sb_sample_kernel, n_pages=n_pages, n_new=n_new),
        grid_spec=grid_spec,
        out_shape=jax.ShapeDtypeStruct((b, SB_KV_HEADS, rows, HEAD_DIM), F32),
        compiler_params=_params(("parallel",)),
        name="sb_sample",
    )(page_table, q, k_new, v_new, *([k_pool] * n_pages), *([v_pool] * n_pages))


def _cmp_sample_kernel(pt_ref, q_ref, kn_ref, vn_ref, w1_ref, b_ref, w2_ref, *refs, n_pages, n_new):
    n_refs = NSA_KV_HEADS * n_pages
    k_pages = refs[:n_refs]
    v_pages = refs[n_refs:2 * n_refs]
    o_ref, bias_ref = refs[2 * n_refs], refs[2 * n_refs + 1]
    per_page = PAGE // CMP_STRIDE
    past = n_pages * PAGE
    n_chunks = n_pages * per_page + 8
    n_cmp = n_chunks - 8
    rows = n_new * NSA_GROUP

    def summaries(pages, new_ref, which, h):
        sl = slice(h * HEAD_DIM, (h + 1) * HEAD_DIM)

        def get(j):
            parts = [r[pl.ds(j, per_page, stride=CMP_STRIDE), :] for r in pages[h * n_pages:(h + 1) * n_pages]]
            parts.append(new_ref[0, j][:, sl])
            return jnp.concatenate(parts, axis=0)

        return _compress(get, w1_ref.at[which], b_ref.at[which], w2_ref.at[which], n_chunks)

    c = lax.broadcasted_iota(jnp.int32, (rows, n_chunks), 1)
    t_row = lax.broadcasted_iota(jnp.int32, (rows, n_chunks), 0) >> NSA_GROUP_SHIFT
    mask = (c * CMP_STRIDE + (CMP_BLOCK - 1) <= past + t_row) & (c < n_cmp)
    cc = lax.broadcasted_iota(jnp.int32, (n_chunks, LANES), 0) * CMP_STRIDE
    s0 = lax.broadcasted_iota(jnp.int32, (n_chunks, LANES), 1) * SLC_BLOCK
    ov = jnp.where((cc < s0 + SLC_BLOCK) & (cc + CMP_BLOCK > s0), 1.0, 0.0).astype(BF16)
    blk = lax.broadcasted_iota(jnp.int32, (n_new, LANES), 1)
    cur = (past + lax.broadcasted_iota(jnp.int32, (n_new, LANES), 0)) >> SLC_SHIFT
    valid = blk <= cur
    forced = valid & ((blk == 0) | (blk > cur - N_LOCAL_BLOCKS))
    n_slc = past // SLC_BLOCK + -(-n_new // SLC_BLOCK)
    for h in range(NSA_KV_HEADS):
        kc = summaries(k_pages, kn_ref, 0, h).astype(BF16)
        vc = summaries(v_pages, vn_ref, 1, h).astype(BF16)
        p = _softmax_rows(_nt(q_ref[0, h], kc) * SCALE, mask)
        o_ref[0, h] = jnp.dot(p.astype(BF16), vc, preferred_element_type=F32)
        psum = jnp.sum(p.reshape(n_new, NSA_GROUP, n_chunks), axis=1)
        hi = psum.astype(BF16)
        lo = (psum - hi.astype(F32)).astype(BF16)
        imp = jnp.dot(hi, ov, preferred_element_type=F32) + jnp.dot(lo, ov, preferred_element_type=F32)
        score = jnp.where(valid, imp + jnp.where(forced, FORCE_BONUS, 0.0), -1.0)
        rank = jnp.zeros((n_new, LANES), F32)
        for m in range(n_slc):
            sm = score[:, m:m + 1]
            rank = rank + jnp.where(blk > m, jnp.where(sm >= score, 1.0, 0.0), jnp.where(sm > score, 1.0, 0.0))
        bias_ref[0, h] = jnp.where(valid & (rank < SLC_TOPK), 0.0, MASK_BIAS)


def cmp_sample(page_table, q, k_new, v_new, w1s, bs, w2s, k_pool, v_pool):
    b, n_pages = page_table.shape
    rows = q.shape[2]
    n_new = rows // NSA_GROUP
    width = NSA_KV_HEADS * HEAD_DIM
    full = lambda a: pl.BlockSpec(a.shape, lambda s, pt: (0,) * a.ndim)
    grid_spec = pltpu.PrefetchScalarGridSpec(
        num_scalar_prefetch=1, grid=(b,),
        in_specs=[pl.BlockSpec((1, NSA_KV_HEADS, rows, HEAD_DIM), lambda s, pt: (s, 0, 0, 0)),
                  pl.BlockSpec((1, CMP_STRIDE, 8, width), lambda s, pt: (s, 0, 0, 0)),
                  pl.BlockSpec((1, CMP_STRIDE, 8, width), lambda s, pt: (s, 0, 0, 0)),
                  full(w1s), full(bs), full(w2s)]
                 + _page_head_specs(n_pages, NSA_KV_HEADS) + _page_head_specs(n_pages, NSA_KV_HEADS),
        out_specs=[pl.BlockSpec((1, NSA_KV_HEADS, rows, HEAD_DIM), lambda s, pt: (s, 0, 0, 0)),
                   pl.BlockSpec((1, NSA_KV_HEADS, n_new, LANES), lambda s, pt: (s, 0, 0, 0))])
    return pl.pallas_call(
        functools.partial(_cmp_sample_kernel, n_pages=n_pages, n_new=n_new),
        grid_spec=grid_spec,
        out_shape=[jax.ShapeDtypeStruct((b, NSA_KV_HEADS, rows, HEAD_DIM), F32),
                   jax.ShapeDtypeStruct((b, NSA_KV_HEADS, n_new, LANES), F32)],
        compiler_params=_params(("parallel",)),
        name="cmp_sample",
    )(page_table, q, k_new, v_new, w1s, bs, w2s,
      *([k_pool] * (NSA_KV_HEADS * n_pages)), *([v_pool] * (NSA_KV_HEADS * n_pages)))


def _slc_sample_kernel(pt_ref, q_ref, bias_ref, kn_ref, vn_ref, *refs, n_pages, n_new):
    k_pages = refs[:n_pages]
    v_pages = refs[n_pages:2 * n_pages]
    o_ref = refs[2 * n_pages]
    past = n_pages * PAGE
    nk = (n_pages + 1) * PAGE
    rows = n_new * NSA_GROUP
    kpos = lax.broadcasted_iota(jnp.int32, (rows, nk), 1)
    qpos = past + (lax.broadcasted_iota(jnp.int32, (rows, nk), 0) >> NSA_GROUP_SHIFT)
    blk_of_key = lax.broadcasted_iota(jnp.int32, (LANES, nk), 1) >> SLC_SHIFT
    expand = jnp.where(blk_of_key == lax.broadcasted_iota(jnp.int32, (LANES, nk), 0), 1.0, 0.0).astype(BF16)
    for h in range(NSA_KV_HEADS):
        kh = _gather_head(k_pages, kn_ref, h)
        vh = _gather_head(v_pages, vn_ref, h)
        bias = bias_ref[0, h]
        s = (_nt(q_ref[0, h], kh) + jnp.dot(bias, expand, preferred_element_type=F32)) * SCALE
        mask = (kpos <= qpos) & (s > 0.5 * MASK_BIAS * SCALE)
        p = _softmax_rows(s, mask)
        o_ref[0, h] = jnp.dot(p.astype(BF16), vh, preferred_element_type=F32)


def slc_sample(page_table, q, bias, k_new, v_new, k_pool, v_pool):
    b, n_pages = page_table.shape
    rows = q.shape[2]
    n_new = rows // NSA_GROUP
    width = NSA_KV_HEADS * HEAD_DIM
    qspec = pl.BlockSpec((1, NSA_KV_HEADS, rows, HEAD_DIM), lambda s, pt: (s, 0, 0, 0))
    grid_spec = pltpu.PrefetchScalarGridSpec(
        num_scalar_prefetch=1, grid=(b,),
        in_specs=[qspec, qspec,
                  pl.BlockSpec((1, PAGE, width), lambda s, pt: (s, 0, 0)),
                  pl.BlockSpec((1, PAGE, width), lambda s, pt: (s, 0, 0))]
                 + _page_specs(n_pages, width) + _page_specs(n_pages, width),
        out_specs=qspec)
    return pl.pallas_call(
        functools.partial(_slc_sample_kernel, n_pages=n_pages, n_new=n_new),
        grid_spec=grid_spec,
        out_shape=jax.ShapeDtypeStruct((b, NSA_KV_HEADS, rows, HEAD_DIM), F32),
        compiler_params=_params(("parallel",)),
        name="slc_sample",
    )(page_table, q, bias, k_new, v_new, *([k_pool] * n_pages), *([v_pool] * n_pages))


def _win_sample_kernel(q_ref, kn_ref, vn_ref, kb_ref, vb_ref, o_ref, ko_ref, vo_ref, *, n_new, past):
    win = kb_ref.shape[1]
    rows = n_new * NSA_GROUP
    nk = win + 8
    kall = jnp.concatenate([kb_ref[0], kn_ref[0]], axis=0)
    vall = jnp.concatenate([vb_ref[0], vn_ref[0]], axis=0)
    ko_ref[0] = kall[n_new:n_new + win]
    vo_ref[0] = vall[n_new:n_new + win]
    kpos = past - win + lax.broadcasted_iota(jnp.int32, (rows, nk), 1)
    qpos = past + (lax.broadcasted_iota(jnp.int32, (rows, nk), 0) >> NSA_GROUP_SHIFT)
    dist = qpos - kpos
    mask = (dist >= 0) & (dist < WINDOW) & (kpos >= 0)
    k16 = kall.astype(BF16)
    v16 = vall.astype(BF16)
    for h in range(NSA_KV_HEADS):
        sl = slice(h * HEAD_DIM, (h + 1) * HEAD_DIM)
        p = _softmax_rows(_nt(q_ref[0, h], k16[:, sl]) * SCALE, mask)
        o_ref[0, h] = jnp.dot(p.astype(BF16), v16[:, sl], preferred_element_type=F32)


def win_sample(q, k_new, v_new, k_buf, v_buf, *, n_new, past):
    b, win, width = k_buf.shape
    rows = q.shape[2]
    qspec = pl.BlockSpec((1, NSA_KV_HEADS, rows, HEAD_DIM), lambda s: (s, 0, 0, 0))
    new = pl.BlockSpec((1, 8, width), lambda s: (s, 0, 0))
    buf = pl.BlockSpec((1, win, width), lambda s: (s, 0, 0))
    return pl.pallas_call(
        functools.partial(_win_sample_kernel, n_new=n_new, past=past),
        grid=(b,),
        in_specs=[qspec, new, new, buf, buf],
        out_specs=[qspec, buf, buf],
        out_shape=[jax.ShapeDtypeStruct((b, NSA_KV_HEADS, rows, HEAD_DIM), F32),
                   jax.ShapeDtypeStruct(k_buf.shape, F32), jax.ShapeDtypeStruct(v_buf.shape, F32)],
        compiler_params=_params(("parallel",)),
        name="win_sample",
    )(q, k_new, v_new, k_buf, v_buf)


def _project(h, w_in):
    proj = matmul(h, w_in, tm=1024, tn=512, n_cols=OFF_GATE, name="in_proj")
    w_gate_logits = jnp.pad(w_in[:, OFF_GATE:], ((0, 0), (0, LANES - N_GATE)))
    logits = matmul(h, w_gate_logits, tm=1024, tn=LANES, name="gate_logits")
    return proj, logits


def _finish_layer(x, mixed, mod, w):
    shift_m, scale_m, gate_m, shift_f, scale_f, gate_f = mod
    o = matmul(mixed, w['w_o'], tm=1024, tn=512, name="out_proj")
    x1, h2 = resid_norm(x, o, w['g_post_mix'], gate_m, (w['g_pre_ffn'], scale_f, shift_f))
    act = gate_up(h2, w['w_gate'], w['w_up'], tm=1024, tn=256)
    f = matmul(act, w['w_down'], tm=512, tn=256, name="ffn_down")
    return resid_norm(x1, f, w['g_post_ffn'], gate_f)[0]


def _prompt_layer(x, mod, w, cmp_w):
    t = x.shape[0]
    shift_m, scale_m = mod[0], mod[1]
    h = norm_mod(x, w['g_pre_mix'], scale_m, shift_m)
    proj, logits = _project(h, w['w_in'])
    (q_sb, q_nsa, q_rot, sbk, sbv, sbk16, sbv16, ck, cv, sk, sv, sk16, sv16,
     wk, wv, wk16, wv16, gates) = prep_heads(proj, logits, jnp.arange(t))
    o_sb = sb_prompt(q_sb, sbk16, sbv16)
    kvc = compress_prompt(ck, cv, *cmp_w)
    o_cmp, bias_t = cmp_prompt(q_nsa, kvc)
    bias = jnp.swapaxes(bias_t, 1, 2).astype(BF16)
    o_slc = slc_prompt(q_rot, bias, sk16, sv16)
    o_win = win_prompt(q_rot, wk16, wv16)
    mixed = combine_heads(o_sb, o_cmp, o_slc, o_win, gates, w['g_out_sb'], w['g_out_nsa'])
    y = _finish_layer(x, mixed, mod, w)
    win = min(WINDOW, t)
    states = (sbk.reshape(1, t, SB_KV_HEADS, HEAD_DIM), sbv.reshape(1, t, SB_KV_HEADS, HEAD_DIM),
              ck.reshape(1, t, NSA_KV_HEADS, HEAD_DIM), cv.reshape(1, t, NSA_KV_HEADS, HEAD_DIM),
              sk.reshape(1, t, NSA_KV_HEADS, HEAD_DIM), sv.reshape(1, t, NSA_KV_HEADS, HEAD_DIM),
              wk[t - win:].reshape(1, win, NSA_KV_HEADS, HEAD_DIM),
              wv[t - win:].reshape(1, win, NSA_KV_HEADS, HEAD_DIM))
    return y, states


def _seq_rows(a, b, n_new, group):
    kvh = a.shape[0] // group
    a = a.reshape(kvh, group, b, n_new, HEAD_DIM)
    return a.transpose(2, 0, 3, 1, 4).reshape(b, kvh, n_new * group, HEAD_DIM)


def _head_major(o, b, n_new, group):
    kvh = o.shape[1]
    o = o.reshape(b, kvh, n_new, group, HEAD_DIM)
    return o.transpose(1, 3, 0, 2, 4).reshape(kvh * group, b * n_new, HEAD_DIM)


def _pad_rows(a, b, n_new, n_rows):
    a = a.reshape(b, n_new, a.shape[-1])
    return jnp.pad(a, ((0, 0), (0, n_rows - n_new), (0, 0)))


def _sample_layer(x, mod, w, cmp_w, caches, page_table):
    b, n_pages = page_table.shape
    m = x.shape[0]
    n_new = m // b
    past = n_pages * PAGE
    sb_k_pool, sb_v_pool, cmp_k_pool, cmp_v_pool, slc_k_pool, slc_v_pool, win_k_buf, win_v_buf = caches
    shift_m, scale_m = mod[0], mod[1]
    h = norm_mod(x, w['g_pre_mix'], scale_m, shift_m)
    proj, logits = _project(h, w['w_in'])
    pos = jnp.tile(past + jnp.arange(n_new), b)
    (q_sb, q_nsa, q_rot, sbk, sbv, sbk16, sbv16, ck, cv, sk, sv, sk16, sv16,
     wk, wv, wk16, wv16, gates) = prep_heads(proj, logits, pos)

    o_sb = sb_sample(page_table, _seq_rows(q_sb, b, n_new, SB_GROUP),
                     _pad_rows(sbk16, b, n_new, PAGE), _pad_rows(sbv16, b, n_new, PAGE),
                     sb_k_pool.reshape(-1, PAGE, SB_KV_HEADS * HEAD_DIM),
                     sb_v_pool.reshape(-1, PAGE, SB_KV_HEADS * HEAD_DIM))

    nsa_w = NSA_KV_HEADS * HEAD_DIM
    chunk_new = lambda a: jnp.pad(_pad_rows(a, b, n_new, CMP_STRIDE)[:, :, None, :],
                                  ((0, 0), (0, 0), (0, 7), (0, 0)))
    o_cmp, bias = cmp_sample(page_table, _seq_rows(q_nsa, b, n_new, NSA_GROUP), chunk_new(ck), chunk_new(cv),
                             *cmp_w, cmp_k_pool.reshape(-1, PAGE, nsa_w), cmp_v_pool.reshape(-1, PAGE, nsa_w))
    bias_rows = jnp.repeat(bias, NSA_GROUP, axis=2).astype(BF16)
    q_rot_s = _seq_rows(q_rot, b, n_new, NSA_GROUP)
    o_slc = slc_sample(page_table, q_rot_s, bias_rows,
                       _pad_rows(sk16, b, n_new, PAGE), _pad_rows(sv16, b, n_new, PAGE),
                       slc_k_pool.reshape(-1, PAGE, nsa_w), slc_v_pool.reshape(-1, PAGE, nsa_w))
    win = win_k_buf.shape[1]
    o_win, win_k_new, win_v_new = win_sample(q_rot_s, _pad_rows(wk, b, n_new, 8), _pad_rows(wv, b, n_new, 8),
                                             win_k_buf.reshape(b, win, nsa_w), win_v_buf.reshape(b, win, nsa_w),
                                             n_new=n_new, past=past)
    mixed = combine_heads(_head_major(o_sb, b, n_new, SB_GROUP), _head_major(o_cmp, b, n_new, NSA_GROUP),
                          _head_major(o_slc, b, n_new, NSA_GROUP), _head_major(o_win, b, n_new, NSA_GROUP),
                          gates, w['g_out_sb'], w['g_out_nsa'])
    y = _finish_layer(x, mixed, mod, w)
    st = lambda a, nh: a.reshape(b, n_new, nh, HEAD_DIM)
    states = (st(sbk, SB_KV_HEADS), st(sbv, SB_KV_HEADS), st(ck, NSA_KV_HEADS), st(cv, NSA_KV_HEADS),
              st(sk, NSA_KV_HEADS), st(sv, NSA_KV_HEADS),
              win_k_new.reshape(b, win, NSA_KV_HEADS, HEAD_DIM), win_v_new.reshape(b, win, NSA_KV_HEADS, HEAD_DIM))
    return y, states


def kernel(x_prompt, x_sample, cache_sb_k, cache_sb_v, cache_cmp_k, cache_cmp_v, cache_slc_k, cache_slc_v, cache_win_k, cache_win_v, page_table, c_prompt, c_sample, w_ada, b_ada, g_pre_mix, w_in, cmp_pe_k, w_cmp_k1, w_cmp_k2, cmp_pe_v, w_cmp_v1, w_cmp_v2, g_out_sb, g_out_nsa, w_o, g_post_mix, g_pre_ffn, w_gate, w_up, w_down, g_post_ffn):
    depth = w_ada.shape[0]
    bp, t, d = x_prompt.shape
    bs, n_new, _ = x_sample.shape
    assert bp == 1
    y_p = x_prompt.reshape(t, d)
    y_s = x_sample.reshape(bs * n_new, d)
    per_layer = []
    for l in range(depth):
        w = {'g_pre_mix': g_pre_mix[l], 'w_in': w_in[l], 'g_out_sb': g_out_sb[l], 'g_out_nsa': g_out_nsa[l],
             'w_o': w_o[l], 'g_post_mix': g_post_mix[l], 'g_pre_ffn': g_pre_ffn[l], 'w_gate': w_gate[l],
             'w_up': w_up[l], 'w_down': w_down[l], 'g_post_ffn': g_post_ffn[l]}
        n_c = bp + bs
        c_all = jnp.pad(jnp.concatenate([c_prompt, c_sample], axis=0), ((0, -n_c % 16), (0, 0)))
        mod = matmul(c_all, w_ada[l], tm=c_all.shape[0], tn=512, bias=b_ada[l], a_silu=True, name="ada_mod")
        mod_p = jnp.split(mod[:bp], N_ADA, axis=-1)
        mod_s = jnp.split(jnp.repeat(mod[bp:n_c], n_new, axis=0), N_ADA, axis=-1)
        w1k, w2k = _cmp_weights(w_cmp_k1[l], w_cmp_k2[l])
        w1v, w2v = _cmp_weights(w_cmp_v1[l], w_cmp_v2[l])
        cmp_w = (jnp.stack([w1k, w1v]),
                 jnp.stack([_pe_bias(cmp_pe_k[l], w_cmp_k1[l]), _pe_bias(cmp_pe_v[l], w_cmp_v1[l])]),
                 jnp.stack([w2k, w2v]))
        y_p, st_p = _prompt_layer(y_p, mod_p, w, cmp_w)
        caches = (cache_sb_k[l], cache_sb_v[l], cache_cmp_k[l], cache_cmp_v[l],
                  cache_slc_k[l], cache_slc_v[l], cache_win_k[l], cache_win_v[l])
        y_s, st_s = _sample_layer(y_s, mod_s, w, cmp_w, caches, page_table)
        per_layer.append(st_p + st_s)
    new = [jnp.stack(s) for s in zip(*per_layer)]
    return (y_p.reshape(bp, t, d), y_s.reshape(bs, n_new, d), *new)
```

```python
import functools

import numpy as np
import jax
import jax.numpy as jnp
from jax import lax
from jax.experimental import pallas as pl
from jax.experimental.pallas import tpu as pltpu

F32 = jnp.float32
BF16 = jnp.bfloat16

HEAD_DIM = 128
SB_HEADS = 16
SB_KV_HEADS = 4
SB_GROUP = SB_HEADS // SB_KV_HEADS
NSA_HEADS = 16
NSA_KV_HEADS = 2
NSA_GROUP = NSA_HEADS // NSA_KV_HEADS
ROPE_DIM = HEAD_DIM // 4
ROPE_THETA = 500000.0
CMP_STRIDE = 16
CMP_BLOCK = 32
CMP_HIDDEN = 2 * HEAD_DIM
SLC_BLOCK = 64
SLC_TOPK = 16
N_LOCAL_BLOCKS = 2
FORCE_BONUS = 1.0e4
WINDOW = 512
RMS_EPS = 1e-6
N_ADA = 6
PAGE = 128
SCALE = HEAD_DIM ** -0.5
SLC_SHIFT = 6
SB_GROUP_SHIFT = 2
NSA_GROUP_SHIFT = 3

LANES = 128
MASK_BIAS = -float(2 ** 20)
NEG = -1.0e30
SB_DEAD = -110.0
VMEM_LIMIT = 56 * 1024 * 1024

OFF_SB_Q = 0
OFF_SB_K = 2048
OFF_SB_V = 2560
OFF_NSA_Q = 3072
OFF_CMP_K = 5120
OFF_CMP_V = 5376
OFF_SLC_K = 5632
OFF_SLC_V = 5888
OFF_WIN_K = 6144
OFF_WIN_V = 6400
OFF_GATE = 6656
N_GATE = 3 * NSA_HEADS


def _params(sem, vmem=VMEM_LIMIT):
    return pltpu.CompilerParams(dimension_semantics=sem, vmem_limit_bytes=vmem)


def _mm_kernel(*refs, nk, a_silu, has_bias):
    if has_bias:
        a_ref, w_ref, b_ref, o_ref, acc_ref = refs
    else:
        a_ref, w_ref, o_ref, acc_ref = refs
        b_ref = None
    a = a_ref[...]
    if a_silu:
        a = a * jax.nn.sigmoid(a)
    p = jnp.dot(a.astype(BF16), w_ref[...].astype(BF16), preferred_element_type=F32)

    def finish(v):
        if has_bias:
            v = v + b_ref[...]
        o_ref[...] = v.astype(o_ref.dtype)

    if nk == 1:
        finish(p)
    else:
        k = pl.program_id(2)

        @pl.when(k == 0)
        def _():
            acc_ref[...] = p

        @pl.when(k > 0)
        def _():
            acc_ref[...] += p

        @pl.when(k == nk - 1)
        def _():
            finish(acc_ref[...])


def matmul(a, w, *, tm, tn, tk=None, n_cols=None, bias=None, a_silu=False, out_dtype=F32, a_single_buffer=False,
           name="matmul"):
    m, kdim = a.shape
    n = w.shape[1] if n_cols is None else n_cols
    tk = kdim if tk is None else tk
    tm = min(tm, m)
    assert m % tm == 0 and n % tn == 0 and kdim % tk == 0
    nk = kdim // tk
    a_mode = {"pipeline_mode": pl.Buffered(1)} if a_single_buffer else {}
    in_specs = [pl.BlockSpec((tm, tk), lambda i, j, k: (i, k), **a_mode),
                pl.BlockSpec((tk, tn), lambda i, j, k: (k, j))]
    args = [a, w]
    if bias is not None:
        in_specs.append(pl.BlockSpec((1, tn), lambda i, j, k: (0, j)))
        args.append(bias.reshape(1, -1))
    acc_shape = (tm, tn) if nk > 1 else (8, LANES)
    return pl.pallas_call(
        functools.partial(_mm_kernel, nk=nk, a_silu=a_silu, has_bias=bias is not None),
        grid=(m // tm, n // tn, nk),
        in_specs=in_specs,
        out_specs=pl.BlockSpec((tm, tn), lambda i, j, k: (i, j)),
        out_shape=jax.ShapeDtypeStruct((m, n), out_dtype),
        scratch_shapes=[pltpu.VMEM(acc_shape, F32)],
        compiler_params=_params(("parallel", "parallel", "arbitrary")),
        name=name,
    )(*args)


def _gate_up_kernel(a_ref, wg_ref, wu_ref, o_ref):
    a = a_ref[...]
    g = jnp.dot(a, wg_ref[...].astype(BF16), preferred_element_type=F32)
    u = jnp.dot(a, wu_ref[...].astype(BF16), preferred_element_type=F32)
    o_ref[...] = (g * jax.nn.sigmoid(g) * u).astype(o_ref.dtype)


def gate_up(a, wg, wu, *, tm, tn):
    m, kdim = a.shape
    n = wg.shape[1]
    tm = min(tm, m)
    assert m % tm == 0 and n % tn == 0
    return pl.pallas_call(
        _gate_up_kernel,
        grid=(m // tm, n // tn),
        in_specs=[pl.BlockSpec((tm, kdim), lambda i, j: (i, 0)),
                  pl.BlockSpec((kdim, tn), lambda i, j: (0, j)),
                  pl.BlockSpec((kdim, tn), lambda i, j: (0, j))],
        out_specs=pl.BlockSpec((tm, tn), lambda i, j: (i, j)),
        out_shape=jax.ShapeDtypeStruct((m, n), BF16),
        compiler_params=_params(("parallel", "parallel")),
        name="ffn_gate_up",
    )(a, wg, wu)


def _rms(x):
    return x * lax.rsqrt(jnp.mean(x * x, axis=-1, keepdims=True) + RMS_EPS)


def _norm_mod_kernel(x_ref, g_ref, sc_ref, sh_ref, h_ref):
    h = _rms(x_ref[...]) * g_ref[...] * (1.0 + sc_ref[...]) + sh_ref[...]
    h_ref[...] = h.astype(h_ref.dtype)


def _mod_spec(mod, tm, d):
    if mod.shape[0] == 1:
        return pl.BlockSpec((1, d), lambda i: (0, 0))
    return pl.BlockSpec((tm, d), lambda i: (i, 0))


def norm_mod(x, g, scale, shift, *, tm=256):
    m, d = x.shape
    tm = min(tm, m)
    return pl.pallas_call(
        _norm_mod_kernel,
        grid=(m // tm,),
        in_specs=[pl.BlockSpec((tm, d), lambda i: (i, 0)),
                  pl.BlockSpec((1, d), lambda i: (0, 0)),
                  _mod_spec(scale, tm, d), _mod_spec(shift, tm, d)],
        out_specs=pl.BlockSpec((tm, d), lambda i: (i, 0)),
        out_shape=jax.ShapeDtypeStruct((m, d), BF16),
        compiler_params=_params(("parallel",)),
        name="norm_mod",
    )(x, g.reshape(1, d), scale, shift)


def _resid_kernel(*refs, with_mod):
    if with_mod:
        x_ref, o_ref, gp_ref, gate_ref, g2_ref, sc_ref, sh_ref, y_ref, h_ref = refs
    else:
        x_ref, o_ref, gp_ref, gate_ref, y_ref = refs
    y = x_ref[...] + gate_ref[...] * (_rms(o_ref[...]) * gp_ref[...])
    y_ref[...] = y
    if with_mod:
        h = _rms(y) * g2_ref[...] * (1.0 + sc_ref[...]) + sh_ref[...]
        h_ref[...] = h.astype(h_ref.dtype)


def resid_norm(x, o, g_post, gate, mod2=None, *, tm=256):
    m, d = x.shape
    if gate.shape[0] != 1:
        tm = tm // 2
    tm = min(tm, m)
    row = pl.BlockSpec((tm, d), lambda i: (i, 0))
    vec = pl.BlockSpec((1, d), lambda i: (0, 0))
    in_specs = [row, row, vec, _mod_spec(gate, tm, d)]
    args = [x, o, g_post.reshape(1, d), gate]
    out_specs = [row]
    out_shape = [jax.ShapeDtypeStruct((m, d), F32)]
    if mod2 is not None:
        g2, sc, sh = mod2
        in_specs += [vec, _mod_spec(sc, tm, d), _mod_spec(sh, tm, d)]
        args += [g2.reshape(1, d), sc, sh]
        out_specs.append(row)
        out_shape.append(jax.ShapeDtypeStruct((m, d), BF16))
    return pl.pallas_call(
        functools.partial(_resid_kernel, with_mod=mod2 is not None),
        grid=(m // tm,),
        in_specs=in_specs, out_specs=out_specs, out_shape=out_shape,
        compiler_params=_params(("parallel",)),
        name="resid_norm",
    )(*args)


def _rope(x, c, sa, sb):
    half = ROPE_DIM // 2
    return x * c + pltpu.roll(x, half, 1) * sa + pltpu.roll(x, HEAD_DIM - half, 1) * sb


def _prep_kernel(proj_ref, lg_ref, c_ref, sa_ref, sb_ref,
                 qsb_ref, qn_ref, qr_ref,
                 sbk_ref, sbv_ref, sbk16_ref, sbv16_ref,
                 ck_ref, cv_ref,
                 sk_ref, sv_ref, sk16_ref, sv16_ref,
                 wk_ref, wv_ref, wk16_ref, wv16_ref,
                 gates_ref):
    c, sa, sb = c_ref[...], sa_ref[...], sb_ref[...]

    def head(off, h):
        return proj_ref[:, off + h * HEAD_DIM: off + (h + 1) * HEAD_DIM]

    for h in range(SB_HEADS):
        qsb_ref[h] = head(OFF_SB_Q, h).astype(BF16)
    for h in range(NSA_HEADS):
        q = head(OFF_NSA_Q, h)
        qn_ref[h] = q.astype(BF16)
        qr_ref[h] = _rope(q, c, sa, sb).astype(BF16)
    for h in range(SB_KV_HEADS):
        sl = slice(h * HEAD_DIM, (h + 1) * HEAD_DIM)
        k = head(OFF_SB_K, h)
        v = head(OFF_SB_V, h)
        sbk_ref[:, h, :] = k
        sbv_ref[:, h, :] = v
        sbk16_ref[:, sl] = k.astype(BF16)
        sbv16_ref[:, sl] = v.astype(BF16)
    for h in range(NSA_KV_HEADS):
        sl = slice(h * HEAD_DIM, (h + 1) * HEAD_DIM)
        ck_ref[:, h, :] = head(OFF_CMP_K, h)
        cv_ref[:, h, :] = head(OFF_CMP_V, h)
        ks = _rope(head(OFF_SLC_K, h), c, sa, sb)
        sk_ref[:, h, :] = ks
        sk16_ref[:, sl] = ks.astype(BF16)
        kw = _rope(head(OFF_WIN_K, h), c, sa, sb)
        wk_ref[:, h, :] = kw
        wk16_ref[:, sl] = kw.astype(BF16)
        vs = head(OFF_SLC_V, h)
        sv_ref[:, h, :] = vs
        sv16_ref[:, sl] = vs.astype(BF16)
        vw = head(OFF_WIN_V, h)
        wv_ref[:, h, :] = vw
        wv16_ref[:, sl] = vw.astype(BF16)
    gates_ref[...] = jax.nn.sigmoid(lg_ref[...])


def rope_tables(pos):
    half = ROPE_DIM // 2
    inv = ROPE_THETA ** (-jnp.arange(half, dtype=F32) / half)
    ang = pos.astype(F32)[:, None] * inv[None, :]
    cos, sin = jnp.cos(ang), jnp.sin(ang)
    n = pos.shape[0]
    rest = HEAD_DIM - ROPE_DIM
    c = jnp.concatenate([cos, cos, jnp.ones((n, rest), F32)], axis=1)
    sa = jnp.concatenate([jnp.zeros((n, half), F32), sin, jnp.zeros((n, rest), F32)], axis=1)
    sb = jnp.concatenate([-sin, jnp.zeros((n, half + rest), F32)], axis=1)
    return c, sa, sb


def prep_heads(proj, logits, pos, *, tm=256):
    m = proj.shape[0]
    tm = min(tm, m)
    c, sa, sb = rope_tables(pos)
    row = lambda w: pl.BlockSpec((tm, w), lambda i: (i, 0))
    hm = pl.BlockSpec((16, tm, HEAD_DIM), lambda i: (0, i, 0))
    nat = lambda nh: pl.BlockSpec((tm, nh, HEAD_DIM), lambda i: (i, 0, 0))
    f = lambda w, dt: jax.ShapeDtypeStruct((m, w), dt)
    fn = lambda nh: jax.ShapeDtypeStruct((m, nh, HEAD_DIM), F32)
    hshape = jax.ShapeDtypeStruct((16, m, HEAD_DIM), BF16)
    out_specs = [hm, hm, hm,
                 nat(4), nat(4), row(512), row(512),
                 nat(2), nat(2),
                 nat(2), nat(2), row(256), row(256),
                 nat(2), nat(2), row(256), row(256),
                 row(LANES)]
    out_shape = [hshape, hshape, hshape,
                 fn(4), fn(4), f(512, BF16), f(512, BF16),
                 fn(2), fn(2),
                 fn(2), fn(2), f(256, BF16), f(256, BF16),
                 fn(2), fn(2), f(256, BF16), f(256, BF16),
                 f(LANES, F32)]
    return pl.pallas_call(
        _prep_kernel,
        grid=(m // tm,),
        in_specs=[row(proj.shape[1]), row(LANES), row(LANES), row(LANES), row(LANES)],
        out_specs=out_specs, out_shape=out_shape,
        compiler_params=_params(("parallel",)),
        name="prep_heads",
    )(proj, logits, c, sa, sb)


def _nt(a, b):
    return lax.dot_general(a, b, (((1,), (1,)), ((), ())), preferred_element_type=F32)


def _cumsum_matrix():
    j = lax.broadcasted_iota(jnp.int32, (2 * LANES, 2 * LANES), 0) & (LANES - 1)
    s = lax.broadcasted_iota(jnp.int32, (2 * LANES, 2 * LANES), 1)
    return jnp.where((s >= LANES) | (j > s), 1.0, 0.0).astype(BF16)


def _split_hi_lo(x):
    hi = x.astype(BF16)
    lo = (x - hi.astype(F32)).astype(BF16)
    return jnp.concatenate([hi, lo], axis=1)


def _sb_block(z, causal, carry, um):
    sp = jnp.maximum(z, 0.0) + jnp.log1p(jnp.exp(-jnp.abs(z)))
    ls = jnp.where(causal, -sp, 0.0)
    cb = jnp.dot(_split_hi_lo(ls), um, preferred_element_type=F32)
    w = jnp.where(causal, jnp.exp(z + ls + cb[:, :LANES] + carry), 0.0)
    return w, carry + cb[:, LANES:]


def _softmax_rows(s, mask):
    s = jnp.where(mask, s, NEG)
    m = jnp.max(s, axis=-1, keepdims=True)
    p = jnp.where(mask, jnp.exp(s - m), 0.0)
    return p / jnp.maximum(jnp.sum(p, axis=-1, keepdims=True), 1e-30)


def _sb_prompt_kernel(q_ref, k_ref, v_ref, o_ref, acc_ref, carry_ref, *, tq):
    qi = pl.program_id(1)
    rows = SB_GROUP * tq
    q = q_ref[...].reshape(rows, HEAD_DIM)
    um = _cumsum_matrix()
    qpos = qi * tq + (lax.broadcasted_iota(jnp.int32, (rows, LANES), 0) & (tq - 1))
    lane = lax.broadcasted_iota(jnp.int32, (rows, LANES), 1)
    acc_ref[...] = jnp.zeros_like(acc_ref)
    carry_ref[...] = jnp.zeros_like(carry_ref)

    def body(state):
        kb, _ = state
        start = pl.multiple_of(kb * LANES, LANES)
        kt = k_ref[pl.ds(start, LANES), :]
        vt = v_ref[pl.ds(start, LANES), :]
        z = _nt(q, kt) * SCALE
        causal = (start + lane) < qpos
        w, carry = _sb_block(z, causal, carry_ref[...], um)
        acc_ref[...] += jnp.dot(w.astype(BF16), vt, preferred_element_type=F32)
        carry_ref[...] = carry
        return kb - 1, jnp.max(carry)

    lax.while_loop(lambda st: (st[0] >= 0) & (st[1] > SB_DEAD), body,
                   ((qi * tq + tq) // LANES - 1, jnp.float32(0.0)))
    o_ref[...] = acc_ref[...].reshape(1, SB_GROUP, tq, HEAD_DIM)


def sb_prompt(q, k16, v16, *, tq=128):
    t = q.shape[1]
    qv = q.reshape(SB_KV_HEADS, SB_GROUP, t, HEAD_DIM)
    out = pl.pallas_call(
        functools.partial(_sb_prompt_kernel, tq=tq),
        grid=(SB_KV_HEADS, t // tq),
        in_specs=[pl.BlockSpec((1, SB_GROUP, tq, HEAD_DIM), lambda h, i: (h, 0, i, 0)),
                  pl.BlockSpec((t, HEAD_DIM), lambda h, i: (0, h)),
                  pl.BlockSpec((t, HEAD_DIM), lambda h, i: (0, h))],
        out_specs=pl.BlockSpec((1, SB_GROUP, tq, HEAD_DIM), lambda h, i: (h, 0, i, 0)),
        out_shape=jax.ShapeDtypeStruct((SB_KV_HEADS, SB_GROUP, t, HEAD_DIM), F32),
        scratch_shapes=[pltpu.VMEM((SB_GROUP * tq, HEAD_DIM), F32),
                        pltpu.VMEM((SB_GROUP * tq, LANES), F32)],
        compiler_params=_params(("parallel", "arbitrary")),
        name="sb_prompt",
    )(qv, k16, v16)
    return out.reshape(SB_HEADS, t, HEAD_DIM)


def _compress(get_chunk_rows, w1_ref, b_ref, w2_ref, n_chunks):
    ab = jnp.zeros((n_chunks, 2 * CMP_HIDDEN), F32)
    for j in range(CMP_STRIDE):
        ab = ab + jnp.dot(get_chunk_rows(j).astype(BF16), w1_ref[j], preferred_element_type=F32)
    first = ab[:, :CMP_HIDDEN]
    second = ab[:, CMP_HIDDEN:]
    nxt = jnp.concatenate([second[1:], second[:1]], axis=0)
    hidden = first + nxt + b_ref[...]
    return jnp.dot(jax.nn.gelu(hidden).astype(BF16), w2_ref[...], preferred_element_type=F32)


def _cmp_weights(w1, w2):
    half = CMP_STRIDE * HEAD_DIM
    a = w1[:half].reshape(CMP_STRIDE, HEAD_DIM, CMP_HIDDEN)
    b = w1[half:].reshape(CMP_STRIDE, HEAD_DIM, CMP_HIDDEN)
    return jnp.concatenate([a, b], axis=-1).astype(BF16), w2.astype(BF16)


def _pe_bias(pe, w1):
    row = jnp.zeros((8, pe.size), F32).at[0].set(pe.reshape(-1))
    return matmul(row, w1, tm=8, tn=CMP_HIDDEN, name="cmp_pe_bias")[:1]


def _compress_prompt_kernel(x_ref, w1_ref, b_ref, w2_ref, o_ref, *, n_chunks):
    for h in range(NSA_KV_HEADS):
        get = lambda j: x_ref[pl.ds(j, n_chunks, stride=CMP_STRIDE), h, :]
        o_ref[0, h] = _compress(get, w1_ref.at[0], b_ref.at[0], w2_ref.at[0], n_chunks).astype(o_ref.dtype)


def compress_prompt(ck, cv, w1s, bs, w2s):
    t = ck.shape[0]
    n_chunks = t // CMP_STRIDE
    x = jnp.stack([ck, cv])
    return pl.pallas_call(
        functools.partial(_compress_prompt_kernel, n_chunks=n_chunks),
        grid=(2,),
        in_specs=[pl.BlockSpec((None, t, NSA_KV_HEADS, HEAD_DIM), lambda a: (a, 0, 0, 0)),
                  pl.BlockSpec((1, CMP_STRIDE, HEAD_DIM, 2 * CMP_HIDDEN), lambda a: (a, 0, 0, 0)),
                  pl.BlockSpec((1, 1, CMP_HIDDEN), lambda a: (a, 0, 0)),
                  pl.BlockSpec((1, CMP_HIDDEN, HEAD_DIM), lambda a: (a, 0, 0))],
        out_specs=pl.BlockSpec((1, NSA_KV_HEADS, n_chunks, HEAD_DIM), lambda a: (a, 0, 0, 0)),
        out_shape=jax.ShapeDtypeStruct((2, NSA_KV_HEADS, n_chunks, HEAD_DIM), BF16),
        compiler_params=_params(("parallel",)),
        name="compress_prompt",
    )(x, w1s, bs, w2s)


def _overlap_t(n_cmp_pad):
    n = lax.broadcasted_iota(jnp.int32, (LANES, n_cmp_pad), 0)
    c0 = lax.broadcasted_iota(jnp.int32, (LANES, n_cmp_pad), 1) * CMP_STRIDE
    s0 = n * SLC_BLOCK
    return jnp.where((c0 < s0 + SLC_BLOCK) & (c0 + CMP_BLOCK > s0), 1.0, 0.0).astype(BF16)


def _select_bias_t(imp_t, qpos_row, score_ref):
    shape = imp_t.shape
    blk = lax.broadcasted_iota(jnp.int32, shape, 0)
    cur = qpos_row >> SLC_SHIFT
    valid = blk <= cur
    forced = valid & ((blk == 0) | (blk > cur - N_LOCAL_BLOCKS))
    score = jnp.where(valid, imp_t + jnp.where(forced, FORCE_BONUS, 0.0), -1.0)
    score_ref[...] = score

    def body(m, rank):
        sm = score_ref[pl.ds(m, 1), :]
        return rank + jnp.where(blk > m, jnp.where(sm >= score, 1.0, 0.0), jnp.where(sm > score, 1.0, 0.0))

    rank = lax.fori_loop(0, shape[0], body, jnp.zeros(shape, F32))
    return jnp.where(valid & (rank < SLC_TOPK), 0.0, MASK_BIAS)


def _cmp_prompt_kernel(q_ref, kc_ref, vc_ref, o_ref, bias_ref, score_ref, *, tq, n_cmp):
    qi = pl.program_id(1)
    rows = NSA_GROUP * tq
    q = q_ref[...].reshape(rows, HEAD_DIM)
    kc = kc_ref[0, 0]
    vc = vc_ref[0, 0]
    n_pad = kc.shape[0]
    s = _nt(q, kc) * SCALE
    qpos = qi * tq + (lax.broadcasted_iota(jnp.int32, (rows, n_pad), 0) & (tq - 1))
    c = lax.broadcasted_iota(jnp.int32, (rows, n_pad), 1)
    mask = (c * CMP_STRIDE + (CMP_BLOCK - 1) <= qpos) & (c < n_cmp)
    p = _softmax_rows(s, mask)
    o = jnp.dot(p.astype(BF16), vc, preferred_element_type=F32)
    o_ref[...] = o.reshape(1, NSA_GROUP, tq, HEAD_DIM)
    psum = p[:tq]
    for g in range(1, NSA_GROUP):
        psum = psum + p[g * tq:(g + 1) * tq]
    hi = psum.astype(BF16)
    lo = (psum - hi.astype(F32)).astype(BF16)
    ov = _overlap_t(n_pad)
    imp_t = _nt(ov, hi) + _nt(ov, lo)
    qpos_row = qi * tq + lax.broadcasted_iota(jnp.int32, (1, tq), 1)
    bias_ref[0] = _select_bias_t(imp_t, qpos_row, score_ref)


def cmp_prompt(q, kvc, *, tq=128):
    t = q.shape[1]
    n_chunks = kvc.shape[2]
    qv = q.reshape(NSA_KV_HEADS, NSA_GROUP, t, HEAD_DIM)
    o, bias_t = pl.pallas_call(
        functools.partial(_cmp_prompt_kernel, tq=tq, n_cmp=n_chunks - 1),
        grid=(NSA_KV_HEADS, t // tq),
        in_specs=[pl.BlockSpec((1, NSA_GROUP, tq, HEAD_DIM), lambda h, i: (h, 0, i, 0)),
                  pl.BlockSpec((1, 1, n_chunks, HEAD_DIM), lambda h, i: (0, h, 0, 0)),
                  pl.BlockSpec((1, 1, n_chunks, HEAD_DIM), lambda h, i: (1, h, 0, 0))],
        out_specs=[pl.BlockSpec((1, NSA_GROUP, tq, HEAD_DIM), lambda h, i: (h, 0, i, 0)),
                   pl.BlockSpec((1, LANES, tq), lambda h, i: (h, 0, i))],
        out_shape=[jax.ShapeDtypeStruct((NSA_KV_HEADS, NSA_GROUP, t, HEAD_DIM), F32),
                   jax.ShapeDtypeStruct((NSA_KV_HEADS, LANES, t), F32)],
        scratch_shapes=[pltpu.VMEM((LANES, tq), F32)],
        compiler_params=_params(("parallel", "parallel")),
        name="cmp_prompt",
    )(qv, kvc, kvc)
    return o.reshape(NSA_HEADS, t, HEAD_DIM), bias_t


def _slc_prompt_kernel(q_ref, bias_ref, k_ref, v_ref, o_ref, m_ref, l_ref, acc_ref, *, tq, tk):
    qi = pl.program_id(1)
    rows = NSA_GROUP * tq
    reps = tk // LANES
    q = q_ref[...].reshape(rows, HEAD_DIM)
    bias = bias_ref[0]
    qa = jnp.concatenate([q, jnp.concatenate([bias] * NSA_GROUP, axis=0)], axis=1)
    blk_row = lax.broadcasted_iota(jnp.int32, (tk, LANES), 0) >> SLC_SHIFT
    blk_col = lax.broadcasted_iota(jnp.int32, (tk, LANES), 1)
    diag = (qi * tq) // tk

    def scores(kb):
        start = pl.multiple_of(kb * tk, tk)
        onehot = jnp.where(blk_row + kb * (tk // SLC_BLOCK) == blk_col, 1.0, 0.0).astype(BF16)
        ka = jnp.concatenate([k_ref[pl.ds(start, tk), :], onehot], axis=1)
        return _nt(qa, ka), v_ref[pl.ds(start, tk), :]

    s, v = scores(diag)
    qpos = qi * tq + (lax.broadcasted_iota(jnp.int32, (rows, tk), 0) & (tq - 1))
    kpos = diag * tk + lax.broadcasted_iota(jnp.int32, (rows, tk), 1)
    s = jnp.where(kpos <= qpos, s, NEG)
    m0 = jnp.broadcast_to(jnp.max(s, axis=1, keepdims=True), (rows, LANES))
    p = jnp.exp((s - pltpu.repeat(m0, reps, axis=1)) * SCALE)
    m_ref[...] = m0
    l_ref[...] = jnp.broadcast_to(jnp.sum(p, axis=1, keepdims=True), (rows, LANES))
    acc_ref[...] = jnp.dot(p.astype(BF16), v, preferred_element_type=F32)

    def body(i, carry):
        s, v = scores(diag - 1 - i)
        m_old = m_ref[...]
        m_new = jnp.maximum(m_old, jnp.max(s, axis=1, keepdims=True))
        alpha = jnp.exp((m_old - m_new) * SCALE)
        p = jnp.exp((s - pltpu.repeat(m_new, reps, axis=1)) * SCALE)
        l_ref[...] = alpha * l_ref[...] + jnp.sum(p, axis=1, keepdims=True)
        acc_ref[...] = alpha * acc_ref[...] + jnp.dot(p.astype(BF16), v, preferred_element_type=F32)
        m_ref[...] = m_new
        return carry

    lax.fori_loop(0, diag, body, 0)
    o = acc_ref[...] / l_ref[...]
    o_ref[...] = o.reshape(1, NSA_GROUP, tq, HEAD_DIM)


def slc_prompt(q, bias, k16, v16, *, tq=128, tk=512):
    t = q.shape[1]
    qv = q.reshape(NSA_KV_HEADS, NSA_GROUP, t, HEAD_DIM)
    rows = NSA_GROUP * tq
    out = pl.pallas_call(
        functools.partial(_slc_prompt_kernel, tq=tq, tk=tk),
        grid=(NSA_KV_HEADS, t // tq),
        in_specs=[pl.BlockSpec((1, NSA_GROUP, tq, HEAD_DIM), lambda h, i: (h, 0, i, 0)),
                  pl.BlockSpec((1, tq, LANES), lambda h, i: (h, i, 0)),
                  pl.BlockSpec((t, HEAD_DIM), lambda h, i: (0, h)),
                  pl.BlockSpec((t, HEAD_DIM), lambda h, i: (0, h))],
        out_specs=pl.BlockSpec((1, NSA_GROUP, tq, HEAD_DIM), lambda h, i: (h, 0, i, 0)),
        out_shape=jax.ShapeDtypeStruct((NSA_KV_HEADS, NSA_GROUP, t, HEAD_DIM), F32),
        scratch_shapes=[pltpu.VMEM((rows, LANES), F32), pltpu.VMEM((rows, LANES), F32),
                        pltpu.VMEM((rows, HEAD_DIM), F32)],
        compiler_params=_params(("parallel", "arbitrary")),
        name="slc_prompt",
    )(qv, bias, k16, v16)
    return out.reshape(NSA_HEADS, t, HEAD_DIM)


def _win_prompt_kernel(q_ref, k_ref, v_ref, o_ref, *, tq):
    qi = pl.program_id(1)
    rows = NSA_GROUP * tq
    span = WINDOW + tq
    q = q_ref[...].reshape(rows, HEAD_DIM)
    start = pl.multiple_of(jnp.maximum(qi * tq - WINDOW, 0), tq)
    s = _nt(q, k_ref[pl.ds(start, span), :])
    qpos = qi * tq + (lax.broadcasted_iota(jnp.int32, (rows, span), 0) & (tq - 1))
    dist = qpos - (start + lax.broadcasted_iota(jnp.int32, (rows, span), 1))
    s = jnp.where((dist >= 0) & (dist < WINDOW), s, NEG)
    p = jnp.exp((s - jnp.max(s, axis=1, keepdims=True)) * SCALE)
    o = jnp.dot(p.astype(BF16), v_ref[pl.ds(start, span), :], preferred_element_type=F32)
    o = o / jnp.sum(p, axis=1, keepdims=True)
    o_ref[...] = o.reshape(1, NSA_GROUP, tq, HEAD_DIM)


def win_prompt(q, k16, v16, *, tq=128):
    t = q.shape[1]
    qv = q.reshape(NSA_KV_HEADS, NSA_GROUP, t, HEAD_DIM)
    rows = NSA_GROUP * tq
    out = pl.pallas_call(
        functools.partial(_win_prompt_kernel, tq=tq),
        grid=(NSA_KV_HEADS, t // tq),
        in_specs=[pl.BlockSpec((1, NSA_GROUP, tq, HEAD_DIM), lambda h, i: (h, 0, i, 0)),
                  pl.BlockSpec((t, HEAD_DIM), lambda h, i: (0, h)),
                  pl.BlockSpec((t, HEAD_DIM), lambda h, i: (0, h))],
        out_specs=pl.BlockSpec((1, NSA_GROUP, tq, HEAD_DIM), lambda h, i: (h, 0, i, 0)),
        out_shape=jax.ShapeDtypeStruct((NSA_KV_HEADS, NSA_GROUP, t, HEAD_DIM), F32),
        compiler_params=_params(("parallel", "arbitrary")),
        name="win_prompt",
    )(qv, k16, v16)
    return out.reshape(NSA_HEADS, t, HEAD_DIM)


def _combine_kernel(osb_ref, oc_ref, os_ref, ow_ref, gates_ref, gsb_ref, gnsa_ref, out_ref):
    gsb = gsb_ref[...]
    gnsa = gnsa_ref[...]
    gates = gates_ref[...]
    for h in range(SB_HEADS):
        out_ref[:, h * HEAD_DIM:(h + 1) * HEAD_DIM] = (_rms(osb_ref[h]) * gsb).astype(out_ref.dtype)
    for h in range(NSA_HEADS):
        o = (gates[:, h:h + 1] * oc_ref[h]
             + gates[:, NSA_HEADS + h:NSA_HEADS + h + 1] * os_ref[h]
             + gates[:, 2 * NSA_HEADS + h:2 * NSA_HEADS + h + 1] * ow_ref[h])
        col = (SB_HEADS + h) * HEAD_DIM
        out_ref[:, col:col + HEAD_DIM] = (_rms(o) * gnsa).astype(out_ref.dtype)


def combine_heads(o_sb, o_cmp, o_slc, o_win, gates, g_sb, g_nsa, *, tm=256):
    m = o_sb.shape[1]
    tm = min(tm, m)
    hm = pl.BlockSpec((16, tm, HEAD_DIM), lambda i: (0, i, 0))
    vec = pl.BlockSpec((1, HEAD_DIM), lambda i: (0, 0))
    width = (SB_HEADS + NSA_HEADS) * HEAD_DIM
    return pl.pallas_call(
        _combine_kernel,
        grid=(m // tm,),
        in_specs=[hm, hm, hm, hm, pl.BlockSpec((tm, LANES), lambda i: (i, 0)), vec, vec],
        out_specs=pl.BlockSpec((tm, width), lambda i: (i, 0)),
        out_shape=jax.ShapeDtypeStruct((m, width), BF16),
        compiler_params=_params(("parallel",)),
        name="combine_heads",
    )(o_sb, o_cmp, o_slc, o_win, gates, g_sb.reshape(1, -1), g_nsa.reshape(1, -1))


def _page_specs(n_pages, n_heads, layer):
    return [pl.BlockSpec((None, None, PAGE, n_heads, HEAD_DIM),
                         functools.partial(lambda p, s, pt: (layer, pt[s, p], 0, 0, 0), p))
            for p in range(n_pages)]


def _gather_head(page_refs, new_ref, h):
    parts = [r[:, h, :].astype(BF16) for r in page_refs]
    parts.append(new_ref[0, :, h * HEAD_DIM:(h + 1) * HEAD_DIM])
    return jnp.concatenate(parts, axis=0)


def _sb_sample_kernel(pt_ref, q_ref, kn_ref, vn_ref, *refs, n_pages, n_new):
    k_pages = refs[:n_pages]
    v_pages = refs[n_pages:2 * n_pages]
    o_ref = refs[2 * n_pages]
    past = n_pages * PAGE
    nb = n_pages + 1
    rows = n_new * SB_GROUP
    um = _cumsum_matrix()
    kpos = lax.broadcasted_iota(jnp.int32, (rows, nb * PAGE), 1)
    qpos = past + (lax.broadcasted_iota(jnp.int32, (rows, nb * PAGE), 0) >> SB_GROUP_SHIFT)
    causal = kpos < qpos
    for h in range(SB_KV_HEADS):
        kh = _gather_head(k_pages, kn_ref, h)
        vh = _gather_head(v_pages, vn_ref, h)
        z = _nt(q_ref[0, h], kh) * SCALE
        carry = jnp.zeros((rows, LANES), F32)
        ws = [None] * nb
        for b in range(nb - 1, -1, -1):
            sl = slice(b * PAGE, (b + 1) * PAGE)
            ws[b], carry = _sb_block(z[:, sl], causal[:, sl], carry, um)
        w = jnp.concatenate(ws, axis=1)
        o_ref[0, h] = jnp.dot(w.astype(BF16), vh, preferred_element_type=F32)


def sb_sample(page_table, q, k_new, v_new, k_pool, v_pool, layer):
    b, n_pages = page_table.shape
    rows = q.shape[2]
    n_new = rows // SB_GROUP
    width = SB_KV_HEADS * HEAD_DIM
    grid_spec = pltpu.PrefetchScalarGridSpec(
        num_scalar_prefetch=1, grid=(b,),
        in_specs=[pl.BlockSpec((1, SB_KV_HEADS, rows, HEAD_DIM), lambda s, pt: (s, 0, 0, 0)),
                  pl.BlockSpec((1, PAGE, width), lambda s, pt: (s, 0, 0)),
                  pl.BlockSpec((1, PAGE, width), lambda s, pt: (s, 0, 0))]
                 + 2 * _page_specs(n_pages, SB_KV_HEADS, layer),
        out_specs=pl.BlockSpec((1, SB_KV_HEADS, rows, HEAD_DIM), lambda s, pt: (s, 0, 0, 0)))
    return pl.pallas_call(
        functools.partial(_sb_sample_kernel, n_pages=n_pages, n_new=n_new),
        grid_spec=grid_spec,
        out_shape=jax.ShapeDtypeStruct((b, SB_KV_HEADS, rows, HEAD_DIM), F32),
        compiler_params=_params(("parallel",)),
        name="sb_sample",
    )(page_table, q, k_new, v_new, *([k_pool] * n_pages), *([v_pool] * n_pages))


def _cmp_sample_kernel(pt_ref, q_ref, kn_ref, vn_ref, w1_ref, b_ref, w2_ref, *refs, n_pages, n_new):
    k_pages = refs[:n_pages]
    v_pages = refs[n_pages:2 * n_pages]
    o_ref, bias_ref = refs[2 * n_pages], refs[2 * n_pages + 1]
    per_page = PAGE // CMP_STRIDE
    past = n_pages * PAGE
    n_chunks = n_pages * per_page + 8
    n_cmp = n_chunks - 8
    rows = n_new * NSA_GROUP

    def summaries(pages, new_ref, which, h):
        sl = slice(h * HEAD_DIM, (h + 1) * HEAD_DIM)

        def get(j):
            parts = [r[pl.ds(j, per_page, stride=CMP_STRIDE), h, :] for r in pages]
            parts.append(new_ref[0, j][:, sl])
            return jnp.concatenate(parts, axis=0)

        return _compress(get, w1_ref.at[which], b_ref.at[which], w2_ref.at[which], n_chunks)

    c = lax.broadcasted_iota(jnp.int32, (rows, n_chunks), 1)
    t_row = lax.broadcasted_iota(jnp.int32, (rows, n_chunks), 0) >> NSA_GROUP_SHIFT
    mask = (c * CMP_STRIDE + (CMP_BLOCK - 1) <= past + t_row) & (c < n_cmp)
    cc = lax.broadcasted_iota(jnp.int32, (n_chunks, LANES), 0) * CMP_STRIDE
    s0 = lax.broadcasted_iota(jnp.int32, (n_chunks, LANES), 1) * SLC_BLOCK
    ov = jnp.where((cc < s0 + SLC_BLOCK) & (cc + CMP_BLOCK > s0), 1.0, 0.0).astype(BF16)
    blk = lax.broadcasted_iota(jnp.int32, (n_new, LANES), 1)
    cur = (past + lax.broadcasted_iota(jnp.int32, (n_new, LANES), 0)) >> SLC_SHIFT
    valid = blk <= cur
    forced = valid & ((blk == 0) | (blk > cur - N_LOCAL_BLOCKS))
    n_slc = past // SLC_BLOCK + -(-n_new // SLC_BLOCK)
    for h in range(NSA_KV_HEADS):
        kc = summaries(k_pages, kn_ref, 0, h).astype(BF16)
        vc = summaries(v_pages, vn_ref, 1, h).astype(BF16)
        p = _softmax_rows(_nt(q_ref[0, h], kc) * SCALE, mask)
        o_ref[0, h] = jnp.dot(p.astype(BF16), vc, preferred_element_type=F32)
        psum = jnp.sum(p.reshape(n_new, NSA_GROUP, n_chunks), axis=1)
        hi = psum.astype(BF16)
        lo = (psum - hi.astype(F32)).astype(BF16)
        imp = jnp.dot(hi, ov, preferred_element_type=F32) + jnp.dot(lo, ov, preferred_element_type=F32)
        score = jnp.where(valid, imp + jnp.where(forced, FORCE_BONUS, 0.0), -1.0)
        rank = jnp.zeros((n_new, LANES), F32)
        for m in range(n_slc):
            sm = score[:, m:m + 1]
            rank = rank + jnp.where(blk > m, jnp.where(sm >= score, 1.0, 0.0), jnp.where(sm > score, 1.0, 0.0))
        bias_ref[0, h] = jnp.where(valid & (rank < SLC_TOPK), 0.0, MASK_BIAS)


def cmp_sample(page_table, q, k_new, v_new, w1s, bs, w2s, k_pool, v_pool, layer):
    b, n_pages = page_table.shape
    rows = q.shape[2]
    n_new = rows // NSA_GROUP
    width = NSA_KV_HEADS * HEAD_DIM
    full = lambda a: pl.BlockSpec(a.shape, lambda s, pt: (0,) * a.ndim)
    grid_spec = pltpu.PrefetchScalarGridSpec(
        num_scalar_prefetch=1, grid=(b,),
        in_specs=[pl.BlockSpec((1, NSA_KV_HEADS, rows, HEAD_DIM), lambda s, pt: (s, 0, 0, 0)),
                  pl.BlockSpec((1, CMP_STRIDE, 8, width), lambda s, pt: (s, 0, 0, 0)),
                  pl.BlockSpec((1, CMP_STRIDE, 8, width), lambda s, pt: (s, 0, 0, 0)),
                  full(w1s), full(bs), full(w2s)]
                 + 2 * _page_specs(n_pages, NSA_KV_HEADS, layer),
        out_specs=[pl.BlockSpec((1, NSA_KV_HEADS, rows, HEAD_DIM), lambda s, pt: (s, 0, 0, 0)),
                   pl.BlockSpec((1, NSA_KV_HEADS, n_new, LANES), lambda s, pt: (s, 0, 0, 0))])
    return pl.pallas_call(
        functools.partial(_cmp_sample_kernel, n_pages=n_pages, n_new=n_new),
        grid_spec=grid_spec,
        out_shape=[jax.ShapeDtypeStruct((b, NSA_KV_HEADS, rows, HEAD_DIM), F32),
                   jax.ShapeDtypeStruct((b, NSA_KV_HEADS, n_new, LANES), F32)],
        compiler_params=_params(("parallel",)),
        name="cmp_sample",
    )(page_table, q, k_new, v_new, w1s, bs, w2s, *([k_pool] * n_pages), *([v_pool] * n_pages))


def _slc_sample_kernel(pt_ref, q_ref, bias_ref, kn_ref, vn_ref, *refs, n_pages, n_new):
    k_pages = refs[:n_pages]
    v_pages = refs[n_pages:2 * n_pages]
    o_ref = refs[2 * n_pages]
    past = n_pages * PAGE
    nk = (n_pages + 1) * PAGE
    rows = n_new * NSA_GROUP
    kpos = lax.broadcasted_iota(jnp.int32, (rows, nk), 1)
    qpos = past + (lax.broadcasted_iota(jnp.int32, (rows, nk), 0) >> NSA_GROUP_SHIFT)
    blk_of_key = lax.broadcasted_iota(jnp.int32, (LANES, nk), 1) >> SLC_SHIFT
    expand = jnp.where(blk_of_key == lax.broadcasted_iota(jnp.int32, (LANES, nk), 0), 1.0, 0.0).astype(BF16)
    for h in range(NSA_KV_HEADS):
        kh = _gather_head(k_pages, kn_ref, h)
        vh = _gather_head(v_pages, vn_ref, h)
        bias = bias_ref[0, h]
        s = (_nt(q_ref[0, h], kh) + jnp.dot(bias, expand, preferred_element_type=F32)) * SCALE
        mask = (kpos <= qpos) & (s > 0.5 * MASK_BIAS * SCALE)
        p = _softmax_rows(s, mask)
        o_ref[0, h] = jnp.dot(p.astype(BF16), vh, preferred_element_type=F32)


def slc_sample(page_table, q, bias, k_new, v_new, k_pool, v_pool, layer):
    b, n_pages = page_table.shape
    rows = q.shape[2]
    n_new = rows // NSA_GROUP
    width = NSA_KV_HEADS * HEAD_DIM
    qspec = pl.BlockSpec((1, NSA_KV_HEADS, rows, HEAD_DIM), lambda s, pt: (s, 0, 0, 0))
    grid_spec = pltpu.PrefetchScalarGridSpec(
        num_scalar_prefetch=1, grid=(b,),
        in_specs=[qspec, qspec,
                  pl.BlockSpec((1, PAGE, width), lambda s, pt: (s, 0, 0)),
                  pl.BlockSpec((1, PAGE, width), lambda s, pt: (s, 0, 0))]
                 + 2 * _page_specs(n_pages, NSA_KV_HEADS, layer),
        out_specs=qspec)
    return pl.pallas_call(
        functools.partial(_slc_sample_kernel, n_pages=n_pages, n_new=n_new),
        grid_spec=grid_spec,
        out_shape=jax.ShapeDtypeStruct((b, NSA_KV_HEADS, rows, HEAD_DIM), F32),
        compiler_params=_params(("parallel",)),
        name="slc_sample",
    )(page_table, q, bias, k_new, v_new, *([k_pool] * n_pages), *([v_pool] * n_pages))


def _win_sample_kernel(q_ref, kn_ref, vn_ref, kb_ref, vb_ref, o_ref, ko_ref, vo_ref, *, n_new, past):
    win = kb_ref.shape[1]
    rows = n_new * NSA_GROUP
    nk = win + 8
    kpos = past - win + lax.broadcasted_iota(jnp.int32, (rows, nk), 1)
    qpos = past + (lax.broadcasted_iota(jnp.int32, (rows, nk), 0) >> NSA_GROUP_SHIFT)
    dist = qpos - kpos
    mask = (dist >= 0) & (dist < WINDOW) & (kpos >= 0)
    for h in range(NSA_KV_HEADS):
        sl = slice(h * HEAD_DIM, (h + 1) * HEAD_DIM)
        kall = jnp.concatenate([kb_ref[0, :, h, :], kn_ref[0][:, sl]], axis=0)
        vall = jnp.concatenate([vb_ref[0, :, h, :], vn_ref[0][:, sl]], axis=0)
        ko_ref[0, :, h, :] = kall[n_new:n_new + win]
        vo_ref[0, :, h, :] = vall[n_new:n_new + win]
        p = _softmax_rows(_nt(q_ref[0, h], kall.astype(BF16)) * SCALE, mask)
        o_ref[0, h] = jnp.dot(p.astype(BF16), vall.astype(BF16), preferred_element_type=F32)


def win_sample(q, k_new, v_new, k_buf, v_buf, layer, *, n_new, past):
    _, b, win, nh, _ = k_buf.shape
    rows = q.shape[2]
    qspec = pl.BlockSpec((1, NSA_KV_HEADS, rows, HEAD_DIM), lambda s: (s, 0, 0, 0))
    new = pl.BlockSpec((1, 8, nh * HEAD_DIM), lambda s: (s, 0, 0))
    buf_in = pl.BlockSpec((None, 1, win, nh, HEAD_DIM), lambda s: (layer, s, 0, 0, 0))
    buf_out = pl.BlockSpec((1, win, nh, HEAD_DIM), lambda s: (s, 0, 0, 0))
    buf_shape = jax.ShapeDtypeStruct((b, win, nh, HEAD_DIM), F32)
    return pl.pallas_call(
        functools.partial(_win_sample_kernel, n_new=n_new, past=past),
        grid=(b,),
        in_specs=[qspec, new, new, buf_in, buf_in],
        out_specs=[qspec, buf_out, buf_out],
        out_shape=[jax.ShapeDtypeStruct((b, NSA_KV_HEADS, rows, HEAD_DIM), F32), buf_shape, buf_shape],
        compiler_params=_params(("parallel",)),
        name="win_sample",
    )(q, k_new, v_new, k_buf, v_buf)


def _project(h, w_in):
    proj = matmul(h, w_in, tm=1024, tn=512, n_cols=OFF_GATE, name="in_proj")
    w_gate_logits = jnp.pad(w_in[:, OFF_GATE:], ((0, 0), (0, LANES - N_GATE)))
    logits = matmul(h, w_gate_logits, tm=1024, tn=LANES, name="gate_logits")
    return proj, logits


def _finish_layer(x, mixed, mod, w):
    shift_m, scale_m, gate_m, shift_f, scale_f, gate_f = mod
    o = matmul(mixed, w['w_o'], tm=1024, tn=512, name="out_proj")
    x1, h2 = resid_norm(x, o, w['g_post_mix'], gate_m, (w['g_pre_ffn'], scale_f, shift_f))
    act = gate_up(h2, w['w_gate'], w['w_up'], tm=1024, tn=256)
    f = matmul(act, w['w_down'], tm=1024, tn=256, a_single_buffer=True, name="ffn_down")
    return resid_norm(x1, f, w['g_post_ffn'], gate_f)[0]


def _prompt_layer(x, mod, w, cmp_w):
    t = x.shape[0]
    shift_m, scale_m = mod[0], mod[1]
    h = norm_mod(x, w['g_pre_mix'], scale_m, shift_m)
    proj, logits = _project(h, w['w_in'])
    (q_sb, q_nsa, q_rot, sbk, sbv, sbk16, sbv16, ck, cv, sk, sv, sk16, sv16,
     wk, wv, wk16, wv16, gates) = prep_heads(proj, logits, jnp.arange(t))
    o_sb = sb_prompt(q_sb, sbk16, sbv16)
    kvc = compress_prompt(ck, cv, *cmp_w)
    o_cmp, bias_t = cmp_prompt(q_nsa, kvc)
    bias = jnp.swapaxes(bias_t, 1, 2).astype(BF16)
    o_slc = slc_prompt(q_rot, bias, sk16, sv16)
    o_win = win_prompt(q_rot, wk16, wv16)
    mixed = combine_heads(o_sb, o_cmp, o_slc, o_win, gates, w['g_out_sb'], w['g_out_nsa'])
    y = _finish_layer(x, mixed, mod, w)
    win = min(WINDOW, t)
    states = tuple(a[None] for a in (sbk, sbv, ck, cv, sk, sv, wk[t - win:], wv[t - win:]))
    return y, states


def _seq_rows(a, b, n_new, group):
    kvh = a.shape[0] // group
    a = a.reshape(kvh, group, b, n_new, HEAD_DIM)
    return a.transpose(2, 0, 3, 1, 4).reshape(b, kvh, n_new * group, HEAD_DIM)


def _head_major(o, b, n_new, group):
    kvh = o.shape[1]
    o = o.reshape(b, kvh, n_new, group, HEAD_DIM)
    return o.transpose(1, 3, 0, 2, 4).reshape(kvh * group, b * n_new, HEAD_DIM)


def _pad_rows(a, b, n_new, n_rows):
    a = a.reshape(b, n_new, -1)
    return jnp.pad(a, ((0, 0), (0, n_rows - n_new), (0, 0)))


def _sample_layer(x, mod, w, cmp_w, caches, page_table, layer):
    b, n_pages = page_table.shape
    m = x.shape[0]
    n_new = m // b
    past = n_pages * PAGE
    sb_k_pool, sb_v_pool, cmp_k_pool, cmp_v_pool, slc_k_pool, slc_v_pool, win_k_buf, win_v_buf = caches
    shift_m, scale_m = mod[0], mod[1]
    h = norm_mod(x, w['g_pre_mix'], scale_m, shift_m)
    proj, logits = _project(h, w['w_in'])
    pos = jnp.tile(past + jnp.arange(n_new), b)
    (q_sb, q_nsa, q_rot, sbk, sbv, sbk16, sbv16, ck, cv, sk, sv, sk16, sv16,
     wk, wv, wk16, wv16, gates) = prep_heads(proj, logits, pos)

    o_sb = sb_sample(page_table, _seq_rows(q_sb, b, n_new, SB_GROUP),
                     _pad_rows(sbk16, b, n_new, PAGE), _pad_rows(sbv16, b, n_new, PAGE),
                     sb_k_pool, sb_v_pool, layer)

    chunk_new = lambda a: jnp.pad(_pad_rows(a, b, n_new, CMP_STRIDE)[:, :, None, :],
                                  ((0, 0), (0, 0), (0, 7), (0, 0)))
    o_cmp, bias = cmp_sample(page_table, _seq_rows(q_nsa, b, n_new, NSA_GROUP), chunk_new(ck), chunk_new(cv),
                             *cmp_w, cmp_k_pool, cmp_v_pool, layer)
    bias_rows = jnp.repeat(bias, NSA_GROUP, axis=2).astype(BF16)
    q_rot_s = _seq_rows(q_rot, b, n_new, NSA_GROUP)
    o_slc = slc_sample(page_table, q_rot_s, bias_rows,
                       _pad_rows(sk16, b, n_new, PAGE), _pad_rows(sv16, b, n_new, PAGE),
                       slc_k_pool, slc_v_pool, layer)
    o_win, win_k_new, win_v_new = win_sample(q_rot_s, _pad_rows(wk, b, n_new, 8), _pad_rows(wv, b, n_new, 8),
                                             win_k_buf, win_v_buf, layer, n_new=n_new, past=past)
    mixed = combine_heads(_head_major(o_sb, b, n_new, SB_GROUP), _head_major(o_cmp, b, n_new, NSA_GROUP),
                          _head_major(o_slc, b, n_new, NSA_GROUP), _head_major(o_win, b, n_new, NSA_GROUP),
                          gates, w['g_out_sb'], w['g_out_nsa'])
    y = _finish_layer(x, mixed, mod, w)
    st = lambda a: a.reshape(b, n_new, *a.shape[1:])
    states = (st(sbk), st(sbv), st(ck), st(cv), st(sk), st(sv), win_k_new, win_v_new)
    return y, states


def kernel(x_prompt, x_sample, cache_sb_k, cache_sb_v, cache_cmp_k, cache_cmp_v, cache_slc_k, cache_slc_v, cache_win_k, cache_win_v, page_table, c_prompt, c_sample, w_ada, b_ada, g_pre_mix, w_in, cmp_pe_k, w_cmp_k1, w_cmp_k2, cmp_pe_v, w_cmp_v1, w_cmp_v2, g_out_sb, g_out_nsa, w_o, g_post_mix, g_pre_ffn, w_gate, w_up, w_down, g_post_ffn):
    depth = w_ada.shape[0]
    bp, t, d = x_prompt.shape
    bs, n_new, _ = x_sample.shape
    assert bp == 1
    y_p = x_prompt.reshape(t, d)
    y_s = x_sample.reshape(bs * n_new, d)
    per_layer = []
    for l in range(depth):
        w = {'g_pre_mix': g_pre_mix[l], 'w_in': w_in[l], 'g_out_sb': g_out_sb[l], 'g_out_nsa': g_out_nsa[l],
             'w_o': w_o[l], 'g_post_mix': g_post_mix[l], 'g_pre_ffn': g_pre_ffn[l], 'w_gate': w_gate[l],
             'w_up': w_up[l], 'w_down': w_down[l], 'g_post_ffn': g_post_ffn[l]}
        n_c = bp + bs
        c_all = jnp.pad(jnp.concatenate([c_prompt, c_sample], axis=0), ((0, -n_c % 16), (0, 0)))
        mod = matmul(c_all, w_ada[l], tm=c_all.shape[0], tn=512, bias=b_ada[l], a_silu=True, name="ada_mod")
        mod_p = jnp.split(mod[:bp], N_ADA, axis=-1)
        mod_s = jnp.split(jnp.repeat(mod[bp:n_c], n_new, axis=0), N_ADA, axis=-1)
        w1k, w2k = _cmp_weights(w_cmp_k1[l], w_cmp_k2[l])
        w1v, w2v = _cmp_weights(w_cmp_v1[l], w_cmp_v2[l])
        cmp_w = (jnp.stack([w1k, w1v]),
                 jnp.stack([_pe_bias(cmp_pe_k[l], w_cmp_k1[l]), _pe_bias(cmp_pe_v[l], w_cmp_v1[l])]),
                 jnp.stack([w2k, w2v]))
        y_p, st_p = _prompt_layer(y_p, mod_p, w, cmp_w)
        caches = (cache_sb_k, cache_sb_v, cache_cmp_k, cache_cmp_v,
                  cache_slc_k, cache_slc_v, cache_win_k, cache_win_v)
        y_s, st_s = _sample_layer(y_s, mod_s, w, cmp_w, caches, page_table, l)
        per_layer.append(st_p + st_s)
    new = [jnp.stack(s) for s in zip(*per_layer)]
    return (y_p.reshape(bp, t, d), y_s.reshape(bs, n_new, d), *new)
```

```python
import functools

import numpy as np
import jax
import jax.numpy as jnp
from jax import lax
from jax.experimental import pallas as pl
from jax.experimental.pallas import tpu as pltpu

F32 = jnp.float32
BF16 = jnp.bfloat16

HEAD_DIM = 128
SB_HEADS = 16
SB_KV_HEADS = 4
SB_GROUP = SB_HEADS // SB_KV_HEADS
NSA_HEADS = 16
NSA_KV_HEADS = 2
NSA_GROUP = NSA_HEADS // NSA_KV_HEADS
ROPE_DIM = HEAD_DIM // 4
ROPE_THETA = 500000.0
CMP_STRIDE = 16
CMP_BLOCK = 32
CMP_HIDDEN = 2 * HEAD_DIM
SLC_BLOCK = 64
SLC_TOPK = 16
N_LOCAL_BLOCKS = 2
FORCE_BONUS = 1.0e4
WINDOW = 512
RMS_EPS = 1e-6
N_ADA = 6
PAGE = 128
SCALE = HEAD_DIM ** -0.5
SLC_SHIFT = 6
SB_GROUP_SHIFT = 2
NSA_GROUP_SHIFT = 3

LANES = 128
MASK_BIAS = -float(2 ** 20)
NEG = -1.0e30
SB_DEAD = -110.0
VMEM_LIMIT = 56 * 1024 * 1024

OFF_SB_Q = 0
OFF_SB_K = 2048
OFF_SB_V = 2560
OFF_NSA_Q = 3072
OFF_CMP_K = 5120
OFF_CMP_V = 5376
OFF_SLC_K = 5632
OFF_SLC_V = 5888
OFF_WIN_K = 6144
OFF_WIN_V = 6400
OFF_GATE = 6656
N_GATE = 3 * NSA_HEADS


def _params(sem, vmem=VMEM_LIMIT):
    return pltpu.CompilerParams(dimension_semantics=sem, vmem_limit_bytes=vmem)


def _mm_kernel(*refs, nk, a_silu, has_bias):
    if has_bias:
        a_ref, w_ref, b_ref, o_ref, acc_ref = refs
    else:
        a_ref, w_ref, o_ref, acc_ref = refs
        b_ref = None
    a = a_ref[...]
    if a_silu:
        a = a * jax.nn.sigmoid(a)
    p = jnp.dot(a.astype(BF16), w_ref[...].astype(BF16), preferred_element_type=F32)

    def finish(v):
        if has_bias:
            v = v + b_ref[...]
        o_ref[...] = v.astype(o_ref.dtype)

    if nk == 1:
        finish(p)
    else:
        k = pl.program_id(2)

        @pl.when(k == 0)
        def _():
            acc_ref[...] = p

        @pl.when(k > 0)
        def _():
            acc_ref[...] += p

        @pl.when(k == nk - 1)
        def _():
            finish(acc_ref[...])


def matmul(a, w, *, tm, tn, tk=None, n_cols=None, bias=None, a_silu=False, out_dtype=F32, a_single_buffer=False,
           name="matmul"):
    m, kdim = a.shape
    n = w.shape[1] if n_cols is None else n_cols
    tk = kdim if tk is None else tk
    tm = min(tm, m)
    assert m % tm == 0 and n % tn == 0 and kdim % tk == 0
    nk = kdim // tk
    a_mode = {"pipeline_mode": pl.Buffered(1)} if a_single_buffer else {}
    in_specs = [pl.BlockSpec((tm, tk), lambda i, j, k: (i, k), **a_mode),
                pl.BlockSpec((tk, tn), lambda i, j, k: (k, j))]
    args = [a, w]
    if bias is not None:
        in_specs.append(pl.BlockSpec((1, tn), lambda i, j, k: (0, j)))
        args.append(bias.reshape(1, -1))
    acc_shape = (tm, tn) if nk > 1 else (8, LANES)
    return pl.pallas_call(
        functools.partial(_mm_kernel, nk=nk, a_silu=a_silu, has_bias=bias is not None),
        grid=(m // tm, n // tn, nk),
        in_specs=in_specs,
        out_specs=pl.BlockSpec((tm, tn), lambda i, j, k: (i, j)),
        out_shape=jax.ShapeDtypeStruct((m, n), out_dtype),
        scratch_shapes=[pltpu.VMEM(acc_shape, F32)],
        compiler_params=_params(("parallel", "parallel", "arbitrary")),
        name=name,
    )(*args)


def _gate_up_kernel(a_ref, wg_ref, wu_ref, o_ref):
    a = a_ref[...]
    g = jnp.dot(a, wg_ref[...].astype(BF16), preferred_element_type=F32)
    u = jnp.dot(a, wu_ref[...].astype(BF16), preferred_element_type=F32)
    o_ref[...] = (g * jax.nn.sigmoid(g) * u).astype(o_ref.dtype)


def gate_up(a, wg, wu, *, tm, tn):
    m, kdim = a.shape
    n = wg.shape[1]
    tm = min(tm, m)
    assert m % tm == 0 and n % tn == 0
    return pl.pallas_call(
        _gate_up_kernel,
        grid=(m // tm, n // tn),
        in_specs=[pl.BlockSpec((tm, kdim), lambda i, j: (i, 0)),
                  pl.BlockSpec((kdim, tn), lambda i, j: (0, j)),
                  pl.BlockSpec((kdim, tn), lambda i, j: (0, j))],
        out_specs=pl.BlockSpec((tm, tn), lambda i, j: (i, j)),
        out_shape=jax.ShapeDtypeStruct((m, n), BF16),
        compiler_params=_params(("parallel", "parallel")),
        name="ffn_gate_up",
    )(a, wg, wu)


def _rms(x):
    return x * lax.rsqrt(jnp.mean(x * x, axis=-1, keepdims=True) + RMS_EPS)


def _norm_mod_kernel(x_ref, g_ref, sc_ref, sh_ref, h_ref):
    h = _rms(x_ref[...]) * g_ref[...] * (1.0 + sc_ref[...]) + sh_ref[...]
    h_ref[...] = h.astype(h_ref.dtype)


def _mod_spec(mod, tm, d):
    if mod.shape[0] == 1:
        return pl.BlockSpec((1, d), lambda i: (0, 0))
    return pl.BlockSpec((tm, d), lambda i: (i, 0))


def norm_mod(x, g, scale, shift, *, tm=256):
    m, d = x.shape
    tm = min(tm, m)
    return pl.pallas_call(
        _norm_mod_kernel,
        grid=(m // tm,),
        in_specs=[pl.BlockSpec((tm, d), lambda i: (i, 0)),
                  pl.BlockSpec((1, d), lambda i: (0, 0)),
                  _mod_spec(scale, tm, d), _mod_spec(shift, tm, d)],
        out_specs=pl.BlockSpec((tm, d), lambda i: (i, 0)),
        out_shape=jax.ShapeDtypeStruct((m, d), BF16),
        compiler_params=_params(("parallel",)),
        name="norm_mod",
    )(x, g.reshape(1, d), scale, shift)


def _resid_kernel(*refs, with_mod):
    if with_mod:
        x_ref, o_ref, gp_ref, gate_ref, g2_ref, sc_ref, sh_ref, y_ref, h_ref = refs
    else:
        x_ref, o_ref, gp_ref, gate_ref, y_ref = refs
    y = x_ref[...] + gate_ref[...] * (_rms(o_ref[...]) * gp_ref[...])
    y_ref[...] = y
    if with_mod:
        h = _rms(y) * g2_ref[...] * (1.0 + sc_ref[...]) + sh_ref[...]
        h_ref[...] = h.astype(h_ref.dtype)


def resid_norm(x, o, g_post, gate, mod2=None, *, tm=256):
    m, d = x.shape
    if gate.shape[0] != 1:
        tm = tm // 2
    tm = min(tm, m)
    row = pl.BlockSpec((tm, d), lambda i: (i, 0))
    vec = pl.BlockSpec((1, d), lambda i: (0, 0))
    in_specs = [row, row, vec, _mod_spec(gate, tm, d)]
    args = [x, o, g_post.reshape(1, d), gate]
    out_specs = [row]
    out_shape = [jax.ShapeDtypeStruct((m, d), F32)]
    if mod2 is not None:
        g2, sc, sh = mod2
        in_specs += [vec, _mod_spec(sc, tm, d), _mod_spec(sh, tm, d)]
        args += [g2.reshape(1, d), sc, sh]
        out_specs.append(row)
        out_shape.append(jax.ShapeDtypeStruct((m, d), BF16))
    return pl.pallas_call(
        functools.partial(_resid_kernel, with_mod=mod2 is not None),
        grid=(m // tm,),
        in_specs=in_specs, out_specs=out_specs, out_shape=out_shape,
        compiler_params=_params(("parallel",)),
        name="resid_norm",
    )(*args)


def _rope(x, c, sa, sb):
    half = ROPE_DIM // 2
    return x * c + pltpu.roll(x, half, 1) * sa + pltpu.roll(x, HEAD_DIM - half, 1) * sb


def _prep_kernel(proj_ref, lg_ref, c_ref, sa_ref, sb_ref,
                 qsb_ref, qn_ref, qr_ref,
                 sbk_ref, sbv_ref, sbk16_ref, sbv16_ref,
                 ck_ref, cv_ref,
                 sk_ref, sv_ref, sk16_ref, sv16_ref,
                 wk_ref, wv_ref, wk16_ref, wv16_ref,
                 gates_ref):
    c, sa, sb = c_ref[...], sa_ref[...], sb_ref[...]

    def head(off, h):
        return proj_ref[:, off + h * HEAD_DIM: off + (h + 1) * HEAD_DIM]

    for h in range(SB_HEADS):
        qsb_ref[h] = head(OFF_SB_Q, h).astype(BF16)
    for h in range(NSA_HEADS):
        q = head(OFF_NSA_Q, h)
        qn_ref[h] = q.astype(BF16)
        qr_ref[h] = _rope(q, c, sa, sb).astype(BF16)
    for h in range(SB_KV_HEADS):
        sl = slice(h * HEAD_DIM, (h + 1) * HEAD_DIM)
        k = head(OFF_SB_K, h)
        v = head(OFF_SB_V, h)
        sbk_ref[:, h, :] = k
        sbv_ref[:, h, :] = v
        sbk16_ref[:, sl] = k.astype(BF16)
        sbv16_ref[:, sl] = v.astype(BF16)
    for h in range(NSA_KV_HEADS):
        sl = slice(h * HEAD_DIM, (h + 1) * HEAD_DIM)
        ck_ref[:, h, :] = head(OFF_CMP_K, h)
        cv_ref[:, h, :] = head(OFF_CMP_V, h)
        ks = _rope(head(OFF_SLC_K, h), c, sa, sb)
        sk_ref[:, h, :] = ks
        sk16_ref[:, sl] = ks.astype(BF16)
        kw = _rope(head(OFF_WIN_K, h), c, sa, sb)
        wk_ref[:, h, :] = kw
        wk16_ref[:, sl] = kw.astype(BF16)
        vs = head(OFF_SLC_V, h)
        sv_ref[:, h, :] = vs
        sv16_ref[:, sl] = vs.astype(BF16)
        vw = head(OFF_WIN_V, h)
        wv_ref[:, h, :] = vw
        wv16_ref[:, sl] = vw.astype(BF16)
    gates_ref[...] = jax.nn.sigmoid(lg_ref[...])


def rope_tables(pos):
    half = ROPE_DIM // 2
    inv = ROPE_THETA ** (-jnp.arange(half, dtype=F32) / half)
    ang = pos.astype(F32)[:, None] * inv[None, :]
    cos, sin = jnp.cos(ang), jnp.sin(ang)
    n = pos.shape[0]
    rest = HEAD_DIM - ROPE_DIM
    c = jnp.concatenate([cos, cos, jnp.ones((n, rest), F32)], axis=1)
    sa = jnp.concatenate([jnp.zeros((n, half), F32), sin, jnp.zeros((n, rest), F32)], axis=1)
    sb = jnp.concatenate([-sin, jnp.zeros((n, half + rest), F32)], axis=1)
    return c, sa, sb


def prep_heads(proj, logits, pos, *, tm=256):
    m = proj.shape[0]
    tm = min(tm, m)
    c, sa, sb = rope_tables(pos)
    row = lambda w: pl.BlockSpec((tm, w), lambda i: (i, 0))
    hm = pl.BlockSpec((16, tm, HEAD_DIM), lambda i: (0, i, 0))
    nat = lambda nh: pl.BlockSpec((tm, nh, HEAD_DIM), lambda i: (i, 0, 0))
    f = lambda w, dt: jax.ShapeDtypeStruct((m, w), dt)
    fn = lambda nh: jax.ShapeDtypeStruct((m, nh, HEAD_DIM), F32)
    hshape = jax.ShapeDtypeStruct((16, m, HEAD_DIM), BF16)
    out_specs = [hm, hm, hm,
                 nat(4), nat(4), row(512), row(512),
                 nat(2), nat(2),
                 nat(2), nat(2), row(256), row(256),
                 nat(2), nat(2), row(256), row(256),
                 row(LANES)]
    out_shape = [hshape, hshape, hshape,
                 fn(4), fn(4), f(512, BF16), f(512, BF16),
                 fn(2), fn(2),
                 fn(2), fn(2), f(256, BF16), f(256, BF16),
                 fn(2), fn(2), f(256, BF16), f(256, BF16),
                 f(LANES, F32)]
    return pl.pallas_call(
        _prep_kernel,
        grid=(m // tm,),
        in_specs=[row(proj.shape[1]), row(LANES), row(LANES), row(LANES), row(LANES)],
        out_specs=out_specs, out_shape=out_shape,
        compiler_params=_params(("parallel",)),
        name="prep_heads",
    )(proj, logits, c, sa, sb)


def _nt(a, b):
    return lax.dot_general(a, b, (((1,), (1,)), ((), ())), preferred_element_type=F32)


def _cumsum_matrix():
    j = lax.broadcasted_iota(jnp.int32, (2 * LANES, 2 * LANES), 0) & (LANES - 1)
    s = lax.broadcasted_iota(jnp.int32, (2 * LANES, 2 * LANES), 1)
    return jnp.where((s >= LANES) | (j > s), 1.0, 0.0).astype(BF16)


def _split_hi_lo(x):
    hi = x.astype(BF16)
    lo = (x - hi.astype(F32)).astype(BF16)
    return jnp.concatenate([hi, lo], axis=1)


def _sb_block(z, causal, carry, um):
    sp = jnp.maximum(z, 0.0) + jnp.log1p(jnp.exp(-jnp.abs(z)))
    ls = jnp.where(causal, -sp, 0.0)
    cb = jnp.dot(_split_hi_lo(ls), um, preferred_element_type=F32)
    w = jnp.where(causal, jnp.exp(z + ls + cb[:, :LANES] + carry), 0.0)
    return w, carry + cb[:, LANES:]


def _softmax_rows(s, mask):
    s = jnp.where(mask, s, NEG)
    m = jnp.max(s, axis=-1, keepdims=True)
    p = jnp.where(mask, jnp.exp(s - m), 0.0)
    return p / jnp.maximum(jnp.sum(p, axis=-1, keepdims=True), 1e-30)


def _sb_prompt_kernel(q_ref, k_ref, v_ref, o_ref, acc_ref, carry_ref, *, tq):
    qi = pl.program_id(1)
    rows = SB_GROUP * tq
    q = q_ref[...].reshape(rows, HEAD_DIM)
    um = _cumsum_matrix()
    qpos = qi * tq + (lax.broadcasted_iota(jnp.int32, (rows, LANES), 0) & (tq - 1))
    lane = lax.broadcasted_iota(jnp.int32, (rows, LANES), 1)
    acc_ref[...] = jnp.zeros_like(acc_ref)
    carry_ref[...] = jnp.zeros_like(carry_ref)

    def body(state):
        kb, _ = state
        start = pl.multiple_of(kb * LANES, LANES)
        kt = k_ref[pl.ds(start, LANES), :]
        vt = v_ref[pl.ds(start, LANES), :]
        z = _nt(q, kt) * SCALE
        causal = (start + lane) < qpos
        w, carry = _sb_block(z, causal, carry_ref[...], um)
        acc_ref[...] += jnp.dot(w.astype(BF16), vt, preferred_element_type=F32)
        carry_ref[...] = carry
        return kb - 1, jnp.max(carry)

    lax.while_loop(lambda st: (st[0] >= 0) & (st[1] > SB_DEAD), body,
                   ((qi * tq + tq) // LANES - 1, jnp.float32(0.0)))
    o_ref[...] = acc_ref[...].reshape(1, SB_GROUP, tq, HEAD_DIM)


def sb_prompt(q, k16, v16, *, tq=128):
    t = q.shape[1]
    qv = q.reshape(SB_KV_HEADS, SB_GROUP, t, HEAD_DIM)
    out = pl.pallas_call(
        functools.partial(_sb_prompt_kernel, tq=tq),
        grid=(SB_KV_HEADS, t // tq),
        in_specs=[pl.BlockSpec((1, SB_GROUP, tq, HEAD_DIM), lambda h, i: (h, 0, i, 0)),
                  pl.BlockSpec((t, HEAD_DIM), lambda h, i: (0, h)),
                  pl.BlockSpec((t, HEAD_DIM), lambda h, i: (0, h))],
        out_specs=pl.BlockSpec((1, SB_GROUP, tq, HEAD_DIM), lambda h, i: (h, 0, i, 0)),
        out_shape=jax.ShapeDtypeStruct((SB_KV_HEADS, SB_GROUP, t, HEAD_DIM), F32),
        scratch_shapes=[pltpu.VMEM((SB_GROUP * tq, HEAD_DIM), F32),
                        pltpu.VMEM((SB_GROUP * tq, LANES), F32)],
        compiler_params=_params(("parallel", "arbitrary")),
        name="sb_prompt",
    )(qv, k16, v16)
    return out.reshape(SB_HEADS, t, HEAD_DIM)


def _compress(get_chunk_rows, w1_ref, b_ref, w2_ref, n_chunks):
    rows = NSA_KV_HEADS * n_chunks
    ab = jnp.zeros((rows, 2 * CMP_HIDDEN), F32)
    for jj in range(CMP_STRIDE // 2):
        x = jnp.concatenate(
            [jnp.concatenate([get_chunk_rows(h, 2 * jj), get_chunk_rows(h, 2 * jj + 1)], axis=1)
             for h in range(NSA_KV_HEADS)], axis=0)
        ab = ab + jnp.dot(x.astype(BF16), w1_ref[jj], preferred_element_type=F32)
    first = ab[:, :CMP_HIDDEN]
    second = ab[:, CMP_HIDDEN:]
    nxt = jnp.concatenate([second[1:], second[:1]], axis=0)
    hidden = first + nxt + b_ref[...]
    return jnp.dot(jax.nn.gelu(hidden).astype(BF16), w2_ref[...], preferred_element_type=F32)


def _cmp_weights(w1, w2):
    half = CMP_STRIDE * HEAD_DIM
    a = w1[:half].reshape(CMP_STRIDE // 2, 2 * HEAD_DIM, CMP_HIDDEN)
    b = w1[half:].reshape(CMP_STRIDE // 2, 2 * HEAD_DIM, CMP_HIDDEN)
    return jnp.concatenate([a, b], axis=-1).astype(BF16), w2.astype(BF16)


def _pe_bias(pe, w1):
    row = jnp.zeros((8, pe.size), F32).at[0].set(pe.reshape(-1))
    return matmul(row, w1, tm=8, tn=CMP_HIDDEN, name="cmp_pe_bias")[:1]


def _compress_prompt_kernel(x_ref, w1_ref, b_ref, w2_ref, o_ref, *, n_chunks):
    get = lambda h, j: x_ref[pl.ds(j, n_chunks, stride=CMP_STRIDE), h, :]
    out = _compress(get, w1_ref.at[0], b_ref.at[0], w2_ref.at[0], n_chunks)
    o_ref[0] = out.reshape(NSA_KV_HEADS, n_chunks, HEAD_DIM).astype(o_ref.dtype)


def compress_prompt(ck, cv, w1s, bs, w2s):
    t = ck.shape[0]
    n_chunks = t // CMP_STRIDE
    x = jnp.stack([ck, cv])
    return pl.pallas_call(
        functools.partial(_compress_prompt_kernel, n_chunks=n_chunks),
        grid=(2,),
        in_specs=[pl.BlockSpec((None, t, NSA_KV_HEADS, HEAD_DIM), lambda a: (a, 0, 0, 0)),
                  pl.BlockSpec((1, CMP_STRIDE // 2, 2 * HEAD_DIM, 2 * CMP_HIDDEN), lambda a: (a, 0, 0, 0)),
                  pl.BlockSpec((1, 1, CMP_HIDDEN), lambda a: (a, 0, 0)),
                  pl.BlockSpec((1, CMP_HIDDEN, HEAD_DIM), lambda a: (a, 0, 0))],
        out_specs=pl.BlockSpec((1, NSA_KV_HEADS, n_chunks, HEAD_DIM), lambda a: (a, 0, 0, 0)),
        out_shape=jax.ShapeDtypeStruct((2, NSA_KV_HEADS, n_chunks, HEAD_DIM), BF16),
        compiler_params=_params(("parallel",)),
        name="compress_prompt",
    )(x, w1s, bs, w2s)


def _overlap_t(n_cmp_pad):
    n = lax.broadcasted_iota(jnp.int32, (LANES, n_cmp_pad), 0)
    c0 = lax.broadcasted_iota(jnp.int32, (LANES, n_cmp_pad), 1) * CMP_STRIDE
    s0 = n * SLC_BLOCK
    return jnp.where((c0 < s0 + SLC_BLOCK) & (c0 + CMP_BLOCK > s0), 1.0, 0.0).astype(BF16)


def _select_bias_t(imp_t, qpos_row, score_ref, n_live):
    shape = imp_t.shape
    blk = lax.broadcasted_iota(jnp.int32, shape, 0)
    cur = qpos_row >> SLC_SHIFT
    valid = blk <= cur
    forced = valid & ((blk == 0) | (blk > cur - N_LOCAL_BLOCKS))
    score = jnp.where(valid, imp_t + jnp.where(forced, FORCE_BONUS, 0.0), -1.0)
    score_ref[...] = score

    def body(m, rank):
        sm = score_ref[pl.ds(m, 1), :]
        return rank + jnp.where(blk > m, jnp.where(sm >= score, 1.0, 0.0), jnp.where(sm > score, 1.0, 0.0))

    rank = lax.fori_loop(0, n_live, body, jnp.zeros(shape, F32))
    return jnp.where(valid & (rank < SLC_TOPK), 0.0, MASK_BIAS)


def _cmp_prompt_kernel(q_ref, kc_ref, vc_ref, o_ref, bias_ref, score_ref, *, tq, n_cmp):
    qi = pl.program_id(1)
    rows = NSA_GROUP * tq
    q = q_ref[...].reshape(rows, HEAD_DIM)
    kc = kc_ref[0, 0]
    vc = vc_ref[0, 0]
    n_pad = kc.shape[0]
    s = _nt(q, kc) * SCALE
    qpos = qi * tq + (lax.broadcasted_iota(jnp.int32, (rows, n_pad), 0) & (tq - 1))
    c = lax.broadcasted_iota(jnp.int32, (rows, n_pad), 1)
    mask = (c * CMP_STRIDE + (CMP_BLOCK - 1) <= qpos) & (c < n_cmp)
    p = _softmax_rows(s, mask)
    o = jnp.dot(p.astype(BF16), vc, preferred_element_type=F32)
    o_ref[...] = o.reshape(1, NSA_GROUP, tq, HEAD_DIM)
    psum = p[:tq]
    for g in range(1, NSA_GROUP):
        psum = psum + p[g * tq:(g + 1) * tq]
    hi = psum.astype(BF16)
    lo = (psum - hi.astype(F32)).astype(BF16)
    ov = _overlap_t(n_pad)
    imp_t = _nt(ov, hi) + _nt(ov, lo)
    qpos_row = qi * tq + lax.broadcasted_iota(jnp.int32, (1, tq), 1)
    n_live = jnp.minimum((qi * tq + tq - 1) // SLC_BLOCK + 1, LANES)
    bias_ref[0] = _select_bias_t(imp_t, qpos_row, score_ref, n_live)


def cmp_prompt(q, kvc, *, tq=128):
    t = q.shape[1]
    n_chunks = kvc.shape[2]
    qv = q.reshape(NSA_KV_HEADS, NSA_GROUP, t, HEAD_DIM)
    o, bias_t = pl.pallas_call(
        functools.partial(_cmp_prompt_kernel, tq=tq, n_cmp=n_chunks - 1),
        grid=(NSA_KV_HEADS, t // tq),
        in_specs=[pl.BlockSpec((1, NSA_GROUP, tq, HEAD_DIM), lambda h, i: (h, 0, i, 0)),
                  pl.BlockSpec((1, 1, n_chunks, HEAD_DIM), lambda h, i: (0, h, 0, 0)),
                  pl.BlockSpec((1, 1, n_chunks, HEAD_DIM), lambda h, i: (1, h, 0, 0))],
        out_specs=[pl.BlockSpec((1, NSA_GROUP, tq, HEAD_DIM), lambda h, i: (h, 0, i, 0)),
                   pl.BlockSpec((1, LANES, tq), lambda h, i: (h, 0, i))],
        out_shape=[jax.ShapeDtypeStruct((NSA_KV_HEADS, NSA_GROUP, t, HEAD_DIM), F32),
                   jax.ShapeDtypeStruct((NSA_KV_HEADS, LANES, t), F32)],
        scratch_shapes=[pltpu.VMEM((LANES, tq), F32)],
        compiler_params=_params(("parallel", "parallel")),
        name="cmp_prompt",
    )(qv, kvc, kvc)
    return o.reshape(NSA_HEADS, t, HEAD_DIM), bias_t


def _slc_prompt_kernel(q_ref, bias_ref, k_ref, v_ref, o_ref, m_ref, l_ref, acc_ref, *, tq, tk):
    qi = pl.program_id(1)
    rows = NSA_GROUP * tq
    reps = tk // LANES
    q = q_ref[...].reshape(rows, HEAD_DIM)
    bias = bias_ref[0]
    qa = jnp.concatenate([q, jnp.concatenate([bias] * NSA_GROUP, axis=0)], axis=1)
    blk_row = lax.broadcasted_iota(jnp.int32, (tk, LANES), 0) >> SLC_SHIFT
    blk_col = lax.broadcasted_iota(jnp.int32, (tk, LANES), 1)
    diag = (qi * tq) // tk

    def scores(kb):
        start = pl.multiple_of(kb * tk, tk)
        onehot = jnp.where(blk_row + kb * (tk // SLC_BLOCK) == blk_col, 1.0, 0.0).astype(BF16)
        ka = jnp.concatenate([k_ref[pl.ds(start, tk), :], onehot], axis=1)
        return _nt(qa, ka), v_ref[pl.ds(start, tk), :]

    s, v = scores(diag)
    qpos = qi * tq + (lax.broadcasted_iota(jnp.int32, (rows, tk), 0) & (tq - 1))
    kpos = diag * tk + lax.broadcasted_iota(jnp.int32, (rows, tk), 1)
    s = jnp.where(kpos <= qpos, s, NEG)
    m0 = jnp.broadcast_to(jnp.max(s, axis=1, keepdims=True), (rows, LANES))
    p = jnp.exp((s - jnp.concatenate([m0] * reps, axis=1)) * SCALE)
    m_ref[...] = m0
    l_ref[...] = jnp.broadcast_to(jnp.sum(p, axis=1, keepdims=True), (rows, LANES))
    acc_ref[...] = jnp.dot(p.astype(BF16), v, preferred_element_type=F32)

    def body(i, carry):
        s, v = scores(diag - 1 - i)
        m_old = m_ref[...]
        m_new = jnp.maximum(m_old, jnp.max(s, axis=1, keepdims=True))
        alpha = jnp.exp((m_old - m_new) * SCALE)
        p = jnp.exp((s - jnp.concatenate([m_new] * reps, axis=1)) * SCALE)
        l_ref[...] = alpha * l_ref[...] + jnp.sum(p, axis=1, keepdims=True)
        acc_ref[...] = alpha * acc_ref[...] + jnp.dot(p.astype(BF16), v, preferred_element_type=F32)
        m_ref[...] = m_new
        return carry

    lax.fori_loop(0, diag, body, 0)
    o = acc_ref[...] / l_ref[...]
    o_ref[...] = o.reshape(1, NSA_GROUP, tq, HEAD_DIM)


def slc_prompt(q, bias, k16, v16, *, tq=128, tk=512):
    t = q.shape[1]
    qv = q.reshape(NSA_KV_HEADS, NSA_GROUP, t, HEAD_DIM)
    rows = NSA_GROUP * tq
    out = pl.pallas_call(
        functools.partial(_slc_prompt_kernel, tq=tq, tk=tk),
        grid=(NSA_KV_HEADS, t // tq),
        in_specs=[pl.BlockSpec((1, NSA_GROUP, tq, HEAD_DIM), lambda h, i: (h, 0, i, 0)),
                  pl.BlockSpec((1, tq, LANES), lambda h, i: (h, i, 0)),
                  pl.BlockSpec((t, HEAD_DIM), lambda h, i: (0, h)),
                  pl.BlockSpec((t, HEAD_DIM), lambda h, i: (0, h))],
        out_specs=pl.BlockSpec((1, NSA_GROUP, tq, HEAD_DIM), lambda h, i: (h, 0, i, 0)),
        out_shape=jax.ShapeDtypeStruct((NSA_KV_HEADS, NSA_GROUP, t, HEAD_DIM), F32),
        scratch_shapes=[pltpu.VMEM((rows, LANES), F32), pltpu.VMEM((rows, LANES), F32),
                        pltpu.VMEM((rows, HEAD_DIM), F32)],
        compiler_params=_params(("parallel", "arbitrary")),
        name="slc_prompt",
    )(qv, bias, k16, v16)
    return out.reshape(NSA_HEADS, t, HEAD_DIM)


def _win_prompt_kernel(q_ref, k_ref, v_ref, o_ref, *, tq):
    qi = pl.program_id(1)
    rows = NSA_GROUP * tq
    span = WINDOW + tq
    q = q_ref[...].reshape(rows, HEAD_DIM)
    start = pl.multiple_of(jnp.maximum(qi * tq - WINDOW, 0), tq)
    s = _nt(q, k_ref[pl.ds(start, span), :])
    qpos = qi * tq + (lax.broadcasted_iota(jnp.int32, (rows, span), 0) & (tq - 1))
    dist = qpos - (start + lax.broadcasted_iota(jnp.int32, (rows, span), 1))
    s = jnp.where((dist >= 0) & (dist < WINDOW), s, NEG)
    p = jnp.exp((s - jnp.max(s, axis=1, keepdims=True)) * SCALE)
    o = jnp.dot(p.astype(BF16), v_ref[pl.ds(start, span), :], preferred_element_type=F32)
    o = o / jnp.sum(p, axis=1, keepdims=True)
    o_ref[...] = o.reshape(1, NSA_GROUP, tq, HEAD_DIM)


def win_prompt(q, k16, v16, *, tq=128):
    t = q.shape[1]
    qv = q.reshape(NSA_KV_HEADS, NSA_GROUP, t, HEAD_DIM)
    rows = NSA_GROUP * tq
    out = pl.pallas_call(
        functools.partial(_win_prompt_kernel, tq=tq),
        grid=(NSA_KV_HEADS, t // tq),
        in_specs=[pl.BlockSpec((1, NSA_GROUP, tq, HEAD_DIM), lambda h, i: (h, 0, i, 0)),
                  pl.BlockSpec((t, HEAD_DIM), lambda h, i: (0, h)),
                  pl.BlockSpec((t, HEAD_DIM), lambda h, i: (0, h))],
        out_specs=pl.BlockSpec((1, NSA_GROUP, tq, HEAD_DIM), lambda h, i: (h, 0, i, 0)),
        out_shape=jax.ShapeDtypeStruct((NSA_KV_HEADS, NSA_GROUP, t, HEAD_DIM), F32),
        compiler_params=_params(("parallel", "arbitrary")),
        name="win_prompt",
    )(qv, k16, v16)
    return out.reshape(NSA_HEADS, t, HEAD_DIM)


def _combine_kernel(osb_ref, oc_ref, os_ref, ow_ref, gates_ref, gsb_ref, gnsa_ref, out_ref):
    gsb = gsb_ref[...]
    gnsa = gnsa_ref[...]
    gates = gates_ref[...]
    for h in range(SB_HEADS):
        out_ref[:, h * HEAD_DIM:(h + 1) * HEAD_DIM] = (_rms(osb_ref[h]) * gsb).astype(out_ref.dtype)
    for h in range(NSA_HEADS):
        o = (gates[:, h:h + 1] * oc_ref[h]
             + gates[:, NSA_HEADS + h:NSA_HEADS + h + 1] * os_ref[h]
             + gates[:, 2 * NSA_HEADS + h:2 * NSA_HEADS + h + 1] * ow_ref[h])
        col = (SB_HEADS + h) * HEAD_DIM
        out_ref[:, col:col + HEAD_DIM] = (_rms(o) * gnsa).astype(out_ref.dtype)


def combine_heads(o_sb, o_cmp, o_slc, o_win, gates, g_sb, g_nsa, *, tm=256):
    m = o_sb.shape[1]
    tm = min(tm, m)
    hm = pl.BlockSpec((16, tm, HEAD_DIM), lambda i: (0, i, 0))
    vec = pl.BlockSpec((1, HEAD_DIM), lambda i: (0, 0))
    width = (SB_HEADS + NSA_HEADS) * HEAD_DIM
    return pl.pallas_call(
        _combine_kernel,
        grid=(m // tm,),
        in_specs=[hm, hm, hm, hm, pl.BlockSpec((tm, LANES), lambda i: (i, 0)), vec, vec],
        out_specs=pl.BlockSpec((tm, width), lambda i: (i, 0)),
        out_shape=jax.ShapeDtypeStruct((m, width), BF16),
        compiler_params=_params(("parallel",)),
        name="combine_heads",
    )(o_sb, o_cmp, o_slc, o_win, gates, g_sb.reshape(1, -1), g_nsa.reshape(1, -1))


def _pool_rows(pool):
    d, n_pool, page, nh, hd = pool.shape
    return pool.reshape(d * n_pool, page * nh, hd)


def _page_specs(n_pages, n_heads, page_offset):
    return [pl.BlockSpec((None, PAGE * n_heads, HEAD_DIM),
                         functools.partial(lambda p, s, pt: (page_offset + pt[s, p], 0, 0), p))
            for p in range(n_pages)]


def _head_rows(ref, h, n_heads, n_rows=PAGE, first=0, every=1):
    return ref[pl.ds(first * n_heads + h, n_rows, stride=every * n_heads), :]


def _gather_head(page_refs, new_ref, h, n_heads):
    parts = [_head_rows(r, h, n_heads).astype(BF16) for r in page_refs]
    parts.append(new_ref[0, :, h * HEAD_DIM:(h + 1) * HEAD_DIM])
    return jnp.concatenate(parts, axis=0)


def _sb_sample_kernel(pt_ref, q_ref, kn_ref, vn_ref, *refs, n_pages, n_new):
    k_pages = refs[:n_pages]
    v_pages = refs[n_pages:2 * n_pages]
    o_ref = refs[2 * n_pages]
    past = n_pages * PAGE
    nb = n_pages + 1
    rows = n_new * SB_GROUP
    grp = SB_KV_HEADS * rows
    um = _cumsum_matrix()
    zs = [_nt(q_ref[0, h], _gather_head(k_pages, kn_ref, h, SB_KV_HEADS)) for h in range(SB_KV_HEADS)]
    z = jnp.concatenate([zh[:, b * PAGE:(b + 1) * PAGE] for b in range(nb) for zh in zs], axis=0) * SCALE
    i = lax.broadcasted_iota(jnp.int32, (nb * grp, LANES), 0)
    assert grp & (grp - 1) == 0, "new tokens per sequence must be a power of two"
    kpos = (i >> (grp.bit_length() - 1)) * PAGE + lax.broadcasted_iota(jnp.int32, (nb * grp, LANES), 1)
    qpos = past + ((i & (rows - 1)) >> SB_GROUP_SHIFT)
    causal = kpos < qpos
    sp = jnp.maximum(z, 0.0) + jnp.log1p(jnp.exp(-jnp.abs(z)))
    ls = jnp.where(causal, -sp, 0.0)
    cb = jnp.dot(_split_hi_lo(ls), um, preferred_element_type=F32)
    carries = [None] * nb
    carry = jnp.zeros((grp, LANES), F32)
    for b in range(nb - 1, -1, -1):
        carries[b] = carry
        carry = carry + cb[b * grp:(b + 1) * grp, LANES:]
    w = jnp.where(causal, jnp.exp(z + ls + cb[:, :LANES] + jnp.concatenate(carries, axis=0)), 0.0)
    for h in range(SB_KV_HEADS):
        wh = jnp.concatenate([w[b * grp + h * rows:b * grp + (h + 1) * rows] for b in range(nb)], axis=1)
        vh = _gather_head(v_pages, vn_ref, h, SB_KV_HEADS)
        o_ref[0, h] = jnp.dot(wh.astype(BF16), vh, preferred_element_type=F32)


def sb_sample(page_table, q, k_new, v_new, k_pool, v_pool, layer):
    b, n_pages = page_table.shape
    rows = q.shape[2]
    n_new = rows // SB_GROUP
    width = SB_KV_HEADS * HEAD_DIM
    grid_spec = pltpu.PrefetchScalarGridSpec(
        num_scalar_prefetch=1, grid=(b,),
        in_specs=[pl.BlockSpec((1, SB_KV_HEADS, rows, HEAD_DIM), lambda s, pt: (s, 0, 0, 0)),
                  pl.BlockSpec((1, PAGE, width), lambda s, pt: (s, 0, 0)),
                  pl.BlockSpec((1, PAGE, width), lambda s, pt: (s, 0, 0))]
                 + 2 * _page_specs(n_pages, SB_KV_HEADS, layer * k_pool.shape[1]),
        out_specs=pl.BlockSpec((1, SB_KV_HEADS, rows, HEAD_DIM), lambda s, pt: (s, 0, 0, 0)))
    return pl.pallas_call(
        functools.partial(_sb_sample_kernel, n_pages=n_pages, n_new=n_new),
        grid_spec=grid_spec,
        out_shape=jax.ShapeDtypeStruct((b, SB_KV_HEADS, rows, HEAD_DIM), F32),
        compiler_params=_params(("parallel",)),
        name="sb_sample",
    )(page_table, q, k_new, v_new, *([_pool_rows(k_pool)] * n_pages), *([_pool_rows(v_pool)] * n_pages))


def _cmp_sample_kernel(pt_ref, q_ref, kn_ref, vn_ref, w1_ref, b_ref, w2_ref, *refs, n_pages, n_new):
    k_pages = refs[:n_pages]
    v_pages = refs[n_pages:2 * n_pages]
    o_ref, bias_ref = refs[2 * n_pages], refs[2 * n_pages + 1]
    per_page = PAGE // CMP_STRIDE
    past = n_pages * PAGE
    n_chunks = n_pages * per_page + 8
    n_cmp = n_chunks - 8
    rows = n_new * NSA_GROUP

    def summaries(pages, new_ref, which):
        def get(h, j):
            parts = [_head_rows(r, h, NSA_KV_HEADS, per_page, first=j, every=CMP_STRIDE) for r in pages]
            parts.append(new_ref[0, j][:, h * HEAD_DIM:(h + 1) * HEAD_DIM])
            return jnp.concatenate(parts, axis=0)

        return _compress(get, w1_ref.at[which], b_ref.at[which], w2_ref.at[which], n_chunks)

    kc_all = summaries(k_pages, kn_ref, 0)
    vc_all = summaries(v_pages, vn_ref, 1)

    c = lax.broadcasted_iota(jnp.int32, (rows, n_chunks), 1)
    t_row = lax.broadcasted_iota(jnp.int32, (rows, n_chunks), 0) >> NSA_GROUP_SHIFT
    mask = (c * CMP_STRIDE + (CMP_BLOCK - 1) <= past + t_row) & (c < n_cmp)
    cc = lax.broadcasted_iota(jnp.int32, (n_chunks, LANES), 0) * CMP_STRIDE
    s0 = lax.broadcasted_iota(jnp.int32, (n_chunks, LANES), 1) * SLC_BLOCK
    ov = jnp.where((cc < s0 + SLC_BLOCK) & (cc + CMP_BLOCK > s0), 1.0, 0.0).astype(BF16)
    blk = lax.broadcasted_iota(jnp.int32, (n_new, LANES), 1)
    cur = (past + lax.broadcasted_iota(jnp.int32, (n_new, LANES), 0)) >> SLC_SHIFT
    valid = blk <= cur
    forced = valid & ((blk == 0) | (blk > cur - N_LOCAL_BLOCKS))
    n_slc = past // SLC_BLOCK + -(-n_new // SLC_BLOCK)
    for h in range(NSA_KV_HEADS):
        kc = kc_all[h * n_chunks:(h + 1) * n_chunks].astype(BF16)
        vc = vc_all[h * n_chunks:(h + 1) * n_chunks].astype(BF16)
        p = _softmax_rows(_nt(q_ref[0, h], kc) * SCALE, mask)
        o_ref[0, h] = jnp.dot(p.astype(BF16), vc, preferred_element_type=F32)
        psum = jnp.sum(p.reshape(n_new, NSA_GROUP, n_chunks), axis=1)
        hi = psum.astype(BF16)
        lo = (psum - hi.astype(F32)).astype(BF16)
        imp = jnp.dot(hi, ov, preferred_element_type=F32) + jnp.dot(lo, ov, preferred_element_type=F32)
        score = jnp.where(valid, imp + jnp.where(forced, FORCE_BONUS, 0.0), -1.0)
        rank = jnp.zeros((n_new, LANES), F32)
        for m in range(n_slc):
            sm = score[:, m:m + 1]
            rank = rank + jnp.where(blk > m, jnp.where(sm >= score, 1.0, 0.0), jnp.where(sm > score, 1.0, 0.0))
        bias_ref[0, h] = jnp.where(valid & (rank < SLC_TOPK), 0.0, MASK_BIAS)


def cmp_sample(page_table, q, k_new, v_new, w1s, bs, w2s, k_pool, v_pool, layer):
    b, n_pages = page_table.shape
    rows = q.shape[2]
    n_new = rows // NSA_GROUP
    width = NSA_KV_HEADS * HEAD_DIM
    full = lambda a: pl.BlockSpec(a.shape, lambda s, pt: (0,) * a.ndim)
    grid_spec = pltpu.PrefetchScalarGridSpec(
        num_scalar_prefetch=1, grid=(b,),
        in_specs=[pl.BlockSpec((1, NSA_KV_HEADS, rows, HEAD_DIM), lambda s, pt: (s, 0, 0, 0)),
                  pl.BlockSpec((1, CMP_STRIDE, 8, width), lambda s, pt: (s, 0, 0, 0)),
                  pl.BlockSpec((1, CMP_STRIDE, 8, width), lambda s, pt: (s, 0, 0, 0)),
                  full(w1s), full(bs), full(w2s)]
                 + 2 * _page_specs(n_pages, NSA_KV_HEADS, layer * k_pool.shape[1]),
        out_specs=[pl.BlockSpec((1, NSA_KV_HEADS, rows, HEAD_DIM), lambda s, pt: (s, 0, 0, 0)),
                   pl.BlockSpec((1, NSA_KV_HEADS, n_new, LANES), lambda s, pt: (s, 0, 0, 0))])
    return pl.pallas_call(
        functools.partial(_cmp_sample_kernel, n_pages=n_pages, n_new=n_new),
        grid_spec=grid_spec,
        out_shape=[jax.ShapeDtypeStruct((b, NSA_KV_HEADS, rows, HEAD_DIM), F32),
                   jax.ShapeDtypeStruct((b, NSA_KV_HEADS, n_new, LANES), F32)],
        compiler_params=_params(("parallel",)),
        name="cmp_sample",
    )(page_table, q, k_new, v_new, w1s, bs, w2s,
      *([_pool_rows(k_pool)] * n_pages), *([_pool_rows(v_pool)] * n_pages))


def _slc_sample_kernel(pt_ref, q_ref, bias_ref, kn_ref, vn_ref, *refs, n_pages, n_new):
    k_pages = refs[:n_pages]
    v_pages = refs[n_pages:2 * n_pages]
    o_ref = refs[2 * n_pages]
    past = n_pages * PAGE
    nk = (n_pages + 1) * PAGE
    rows = n_new * NSA_GROUP
    kpos = lax.broadcasted_iota(jnp.int32, (rows, nk), 1)
    qpos = past + (lax.broadcasted_iota(jnp.int32, (rows, nk), 0) >> NSA_GROUP_SHIFT)
    blk_of_key = lax.broadcasted_iota(jnp.int32, (LANES, nk), 1) >> SLC_SHIFT
    expand = jnp.where(blk_of_key == lax.broadcasted_iota(jnp.int32, (LANES, nk), 0), 1.0, 0.0).astype(BF16)
    for h in range(NSA_KV_HEADS):
        kh = _gather_head(k_pages, kn_ref, h, NSA_KV_HEADS)
        vh = _gather_head(v_pages, vn_ref, h, NSA_KV_HEADS)
        bias = bias_ref[0, h]
        s = (_nt(q_ref[0, h], kh) + jnp.dot(bias, expand, preferred_element_type=F32)) * SCALE
        mask = (kpos <= qpos) & (s > 0.5 * MASK_BIAS * SCALE)
        p = _softmax_rows(s, mask)
        o_ref[0, h] = jnp.dot(p.astype(BF16), vh, preferred_element_type=F32)


def slc_sample(page_table, q, bias, k_new, v_new, k_pool, v_pool, layer):
    b, n_pages = page_table.shape
    rows = q.shape[2]
    n_new = rows // NSA_GROUP
    width = NSA_KV_HEADS * HEAD_DIM
    qspec = pl.BlockSpec((1, NSA_KV_HEADS, rows, HEAD_DIM), lambda s, pt: (s, 0, 0, 0))
    grid_spec = pltpu.PrefetchScalarGridSpec(
        num_scalar_prefetch=1, grid=(b,),
        in_specs=[qspec, qspec,
                  pl.BlockSpec((1, PAGE, width), lambda s, pt: (s, 0, 0)),
                  pl.BlockSpec((1, PAGE, width), lambda s, pt: (s, 0, 0))]
                 + 2 * _page_specs(n_pages, NSA_KV_HEADS, layer * k_pool.shape[1]),
        out_specs=qspec)
    return pl.pallas_call(
        functools.partial(_slc_sample_kernel, n_pages=n_pages, n_new=n_new),
        grid_spec=grid_spec,
        out_shape=jax.ShapeDtypeStruct((b, NSA_KV_HEADS, rows, HEAD_DIM), F32),
        compiler_params=_params(("parallel",)),
        name="slc_sample",
    )(page_table, q, bias, k_new, v_new, *([_pool_rows(k_pool)] * n_pages), *([_pool_rows(v_pool)] * n_pages))


def _win_sample_kernel(q_ref, kn_ref, vn_ref, kb_ref, vb_ref, o_ref, ko_ref, vo_ref, *, n_new, past):
    nh = NSA_KV_HEADS
    win = kb_ref.shape[0] // nh
    rows = n_new * NSA_GROUP
    nk = win + 8
    kpos = past - win + lax.broadcasted_iota(jnp.int32, (rows, nk), 1)
    qpos = past + (lax.broadcasted_iota(jnp.int32, (rows, nk), 0) >> NSA_GROUP_SHIFT)
    dist = qpos - kpos
    mask = (dist >= 0) & (dist < WINDOW) & (kpos >= 0)
    for h in range(nh):
        sl = slice(h * HEAD_DIM, (h + 1) * HEAD_DIM)
        kall = jnp.concatenate([_head_rows(kb_ref, h, nh, win), kn_ref[0][:, sl]], axis=0)
        vall = jnp.concatenate([_head_rows(vb_ref, h, nh, win), vn_ref[0][:, sl]], axis=0)
        ko_ref[pl.ds(h, win, stride=nh), :] = kall[n_new:n_new + win]
        vo_ref[pl.ds(h, win, stride=nh), :] = vall[n_new:n_new + win]
        p = _softmax_rows(_nt(q_ref[0, h], kall.astype(BF16)) * SCALE, mask)
        o_ref[0, h] = jnp.dot(p.astype(BF16), vall.astype(BF16), preferred_element_type=F32)


def win_sample(q, k_new, v_new, k_buf, v_buf, layer, *, n_new, past):
    _, b, win, nh, _ = k_buf.shape
    rows = q.shape[2]
    qspec = pl.BlockSpec((1, NSA_KV_HEADS, rows, HEAD_DIM), lambda s: (s, 0, 0, 0))
    new = pl.BlockSpec((1, 8, nh * HEAD_DIM), lambda s: (s, 0, 0))
    buf_in = pl.BlockSpec((None, win * nh, HEAD_DIM), lambda s: (layer * b + s, 0, 0))
    buf_out = pl.BlockSpec((None, win * nh, HEAD_DIM), lambda s: (s, 0, 0))
    buf_shape = jax.ShapeDtypeStruct((b, win * nh, HEAD_DIM), F32)
    o, k_out, v_out = pl.pallas_call(
        functools.partial(_win_sample_kernel, n_new=n_new, past=past),
        grid=(b,),
        in_specs=[qspec, new, new, buf_in, buf_in],
        out_specs=[qspec, buf_out, buf_out],
        out_shape=[jax.ShapeDtypeStruct((b, NSA_KV_HEADS, rows, HEAD_DIM), F32), buf_shape, buf_shape],
        compiler_params=_params(("parallel",)),
        name="win_sample",
    )(q, k_new, v_new, _pool_rows(k_buf), _pool_rows(v_buf))
    return o, k_out.reshape(b, win, nh, HEAD_DIM), v_out.reshape(b, win, nh, HEAD_DIM)


def _project(h, w_in):
    proj = matmul(h, w_in, tm=1024, tn=512, n_cols=OFF_GATE, name="in_proj")
    w_gate_logits = jnp.pad(w_in[:, OFF_GATE:], ((0, 0), (0, LANES - N_GATE)))
    logits = matmul(h, w_gate_logits, tm=1024, tn=LANES, name="gate_logits")
    return proj, logits


def _finish_layer(x, mixed, mod, w):
    shift_m, scale_m, gate_m, shift_f, scale_f, gate_f = mod
    o = matmul(mixed, w['w_o'], tm=1024, tn=512, name="out_proj")
    x1, h2 = resid_norm(x, o, w['g_post_mix'], gate_m, (w['g_pre_ffn'], scale_f, shift_f))
    act = gate_up(h2, w['w_gate'], w['w_up'], tm=1024, tn=256)
    f = matmul(act, w['w_down'], tm=1024, tn=256, a_single_buffer=True, name="ffn_down")
    return resid_norm(x1, f, w['g_post_ffn'], gate_f)[0]


def _prompt_layer(x, mod, w, cmp_w):
    t = x.shape[0]
    shift_m, scale_m = mod[0], mod[1]
    h = norm_mod(x, w['g_pre_mix'], scale_m, shift_m)
    proj, logits = _project(h, w['w_in'])
    (q_sb, q_nsa, q_rot, sbk, sbv, sbk16, sbv16, ck, cv, sk, sv, sk16, sv16,
     wk, wv, wk16, wv16, gates) = prep_heads(proj, logits, jnp.arange(t))
    o_sb = sb_prompt(q_sb, sbk16, sbv16)
    kvc = compress_prompt(ck, cv, *cmp_w)
    o_cmp, bias_t = cmp_prompt(q_nsa, kvc)
    bias = jnp.swapaxes(bias_t, 1, 2).astype(BF16)
    o_slc = slc_prompt(q_rot, bias, sk16, sv16)
    o_win = win_prompt(q_rot, wk16, wv16)
    mixed = combine_heads(o_sb, o_cmp, o_slc, o_win, gates, w['g_out_sb'], w['g_out_nsa'])
    y = _finish_layer(x, mixed, mod, w)
    win = min(WINDOW, t)
    states = tuple(a[None] for a in (sbk, sbv, ck, cv, sk, sv, wk[t - win:], wv[t - win:]))
    return y, states


def _seq_rows(a, b, n_new, group):
    kvh = a.shape[0] // group
    a = a.reshape(kvh, group, b, n_new, HEAD_DIM)
    return a.transpose(2, 0, 3, 1, 4).reshape(b, kvh, n_new * group, HEAD_DIM)


def _head_major(o, b, n_new, group):
    kvh = o.shape[1]
    o = o.reshape(b, kvh, n_new, group, HEAD_DIM)
    return o.transpose(1, 3, 0, 2, 4).reshape(kvh * group, b * n_new, HEAD_DIM)


def _pad_rows(a, b, n_new, n_rows):
    a = a.reshape(b, n_new, -1)
    return jnp.pad(a, ((0, 0), (0, n_rows - n_new), (0, 0)))


def _sample_layer(x, mod, w, cmp_w, caches, page_table, layer):
    b, n_pages = page_table.shape
    m = x.shape[0]
    n_new = m // b
    past = n_pages * PAGE
    sb_k_pool, sb_v_pool, cmp_k_pool, cmp_v_pool, slc_k_pool, slc_v_pool, win_k_buf, win_v_buf = caches
    shift_m, scale_m = mod[0], mod[1]
    h = norm_mod(x, w['g_pre_mix'], scale_m, shift_m)
    proj, logits = _project(h, w['w_in'])
    pos = jnp.tile(past + jnp.arange(n_new), b)
    (q_sb, q_nsa, q_rot, sbk, sbv, sbk16, sbv16, ck, cv, sk, sv, sk16, sv16,
     wk, wv, wk16, wv16, gates) = prep_heads(proj, logits, pos)

    o_sb = sb_sample(page_table, _seq_rows(q_sb, b, n_new, SB_GROUP),
                     _pad_rows(sbk16, b, n_new, PAGE), _pad_rows(sbv16, b, n_new, PAGE),
                     sb_k_pool, sb_v_pool, layer)

    chunk_new = lambda a: jnp.pad(_pad_rows(a, b, n_new, CMP_STRIDE)[:, :, None, :],
                                  ((0, 0), (0, 0), (0, 7), (0, 0)))
    o_cmp, bias = cmp_sample(page_table, _seq_rows(q_nsa, b, n_new, NSA_GROUP), chunk_new(ck), chunk_new(cv),
                             *cmp_w, cmp_k_pool, cmp_v_pool, layer)
    bias_rows = jnp.repeat(bias, NSA_GROUP, axis=2).astype(BF16)
    q_rot_s = _seq_rows(q_rot, b, n_new, NSA_GROUP)
    o_slc = slc_sample(page_table, q_rot_s, bias_rows,
                       _pad_rows(sk16, b, n_new, PAGE), _pad_rows(sv16, b, n_new, PAGE),
                       slc_k_pool, slc_v_pool, layer)
    o_win, win_k_new, win_v_new = win_sample(q_rot_s, _pad_rows(wk, b, n_new, 8), _pad_rows(wv, b, n_new, 8),
                                             win_k_buf, win_v_buf, layer, n_new=n_new, past=past)
    mixed = combine_heads(_head_major(o_sb, b, n_new, SB_GROUP), _head_major(o_cmp, b, n_new, NSA_GROUP),
                          _head_major(o_slc, b, n_new, NSA_GROUP), _head_major(o_win, b, n_new, NSA_GROUP),
                          gates, w['g_out_sb'], w['g_out_nsa'])
    y = _finish_layer(x, mixed, mod, w)
    st = lambda a: a.reshape(b, n_new, *a.shape[1:])
    states = (st(sbk), st(sbv), st(ck), st(cv), st(sk), st(sv), win_k_new, win_v_new)
    return y, states


def kernel(x_prompt, x_sample, cache_sb_k, cache_sb_v, cache_cmp_k, cache_cmp_v, cache_slc_k, cache_slc_v, cache_win_k, cache_win_v, page_table, c_prompt, c_sample, w_ada, b_ada, g_pre_mix, w_in, cmp_pe_k, w_cmp_k1, w_cmp_k2, cmp_pe_v, w_cmp_v1, w_cmp_v2, g_out_sb, g_out_nsa, w_o, g_post_mix, g_pre_ffn, w_gate, w_up, w_down, g_post_ffn):
    depth = w_ada.shape[0]
    bp, t, d = x_prompt.shape
    bs, n_new, _ = x_sample.shape
    assert bp == 1
    y_p = x_prompt.reshape(t, d)
    y_s = x_sample.reshape(bs * n_new, d)
    per_layer = []
    for l in range(depth):
        w = {'g_pre_mix': g_pre_mix[l], 'w_in': w_in[l], 'g_out_sb': g_out_sb[l], 'g_out_nsa': g_out_nsa[l],
             'w_o': w_o[l], 'g_post_mix': g_post_mix[l], 'g_pre_ffn': g_pre_ffn[l], 'w_gate': w_gate[l],
             'w_up': w_up[l], 'w_down': w_down[l], 'g_post_ffn': g_post_ffn[l]}
        n_c = bp + bs
        c_all = jnp.pad(jnp.concatenate([c_prompt, c_sample], axis=0), ((0, -n_c % 16), (0, 0)))
        mod = matmul(c_all, w_ada[l], tm=c_all.shape[0], tn=512, bias=b_ada[l], a_silu=True, name="ada_mod")
        mod_p = jnp.split(mod[:bp], N_ADA, axis=-1)
        mod_s = jnp.split(jnp.repeat(mod[bp:n_c], n_new, axis=0), N_ADA, axis=-1)
        w1k, w2k = _cmp_weights(w_cmp_k1[l], w_cmp_k2[l])
        w1v, w2v = _cmp_weights(w_cmp_v1[l], w_cmp_v2[l])
        cmp_w = (jnp.stack([w1k, w1v]),
                 jnp.stack([_pe_bias(cmp_pe_k[l], w_cmp_k1[l]), _pe_bias(cmp_pe_v[l], w_cmp_v1[l])]),
                 jnp.stack([w2k, w2v]))
        y_p, st_p = _prompt_layer(y_p, mod_p, w, cmp_w)
        caches = (cache_sb_k, cache_sb_v, cache_cmp_k, cache_cmp_v,
                  cache_slc_k, cache_slc_v, cache_win_k, cache_win_v)
        y_s, st_s = _sample_layer(y_s, mod_s, w, cmp_w, caches, page_table, l)
        per_layer.append(st_p + st_s)
    new = [jnp.stack(s) for s in zip(*per_layer)]
    return (y_p.reshape(bp, t, d), y_s.reshape(bs, n_new, d), *new)
```

```python
import functools

import numpy as np
import jax
import jax.numpy as jnp
from jax import lax
from jax.experimental import pallas as pl
from jax.experimental.pallas import tpu as pltpu

F32 = jnp.float32
BF16 = jnp.bfloat16

HEAD_DIM = 128
SB_HEADS = 16
SB_KV_HEADS = 4
SB_GROUP = SB_HEADS // SB_KV_HEADS
NSA_HEADS = 16
NSA_KV_HEADS = 2
NSA_GROUP = NSA_HEADS // NSA_KV_HEADS
ROPE_DIM = HEAD_DIM // 4
ROPE_THETA = 500000.0
CMP_STRIDE = 16
CMP_BLOCK = 32
CMP_HIDDEN = 2 * HEAD_DIM
SLC_BLOCK = 64
SLC_TOPK = 16
N_LOCAL_BLOCKS = 2
FORCE_BONUS = 1.0e4
WINDOW = 512
RMS_EPS = 1e-6
N_ADA = 6
PAGE = 128
SCALE = HEAD_DIM ** -0.5
SLC_SHIFT = 6
SB_GROUP_SHIFT = 2
NSA_GROUP_SHIFT = 3

LANES = 128
MASK_BIAS = -float(2 ** 20)
NEG = -1.0e30
SB_DEAD = -110.0
VMEM_LIMIT = 56 * 1024 * 1024

OFF_SB_Q = 0
OFF_SB_K = 2048
OFF_SB_V = 2560
OFF_NSA_Q = 3072
OFF_CMP_K = 5120
OFF_CMP_V = 5376
OFF_SLC_K = 5632
OFF_SLC_V = 5888
OFF_WIN_K = 6144
OFF_WIN_V = 6400
OFF_GATE = 6656
N_GATE = 3 * NSA_HEADS


def _params(sem, vmem=VMEM_LIMIT):
    return pltpu.CompilerParams(dimension_semantics=sem, vmem_limit_bytes=vmem)


def _nt(a, b):
    return lax.dot_general(a, b, (((1,), (1,)), ((), ())), preferred_element_type=F32)


def _mm_kernel(*refs, nk, a_silu, has_bias, w_rows_are_outputs):
    if has_bias:
        a_ref, w_ref, b_ref, o_ref, acc_ref = refs
    else:
        a_ref, w_ref, o_ref, acc_ref = refs
        b_ref = None
    a = a_ref[...]
    if a_silu:
        a = a * jax.nn.sigmoid(a)
    if w_rows_are_outputs:
        p = _nt(a.astype(BF16), w_ref[...].astype(BF16))
    else:
        p = jnp.dot(a.astype(BF16), w_ref[...].astype(BF16), preferred_element_type=F32)

    def finish(v):
        if has_bias:
            v = v + b_ref[...]
        o_ref[...] = v.astype(o_ref.dtype)

    if nk == 1:
        finish(p)
    else:
        k = pl.program_id(2)

        @pl.when(k == 0)
        def _():
            acc_ref[...] = p

        @pl.when(k > 0)
        def _():
            acc_ref[...] += p

        @pl.when(k == nk - 1)
        def _():
            finish(acc_ref[...])


def matmul(a, w, *, tm, tn, tk=None, n_cols=None, bias=None, a_silu=False, out_dtype=F32, a_single_buffer=False,
           w_rows_are_outputs=False, name="matmul"):
    m, kdim = a.shape
    n = w.shape[0 if w_rows_are_outputs else 1] if n_cols is None else n_cols
    tk = kdim if tk is None else tk
    tm = min(tm, m)
    assert m % tm == 0 and n % tn == 0 and kdim % tk == 0
    nk = kdim // tk
    a_mode = {"pipeline_mode": pl.Buffered(1)} if a_single_buffer else {}
    w_spec = (pl.BlockSpec((tn, tk), lambda i, j, k: (j, k)) if w_rows_are_outputs
              else pl.BlockSpec((tk, tn), lambda i, j, k: (k, j)))
    in_specs = [pl.BlockSpec((tm, tk), lambda i, j, k: (i, k), **a_mode), w_spec]
    args = [a, w]
    if bias is not None:
        in_specs.append(pl.BlockSpec((1, tn), lambda i, j, k: (0, j)))
        args.append(bias.reshape(1, -1))
    acc_shape = (tm, tn) if nk > 1 else (8, LANES)
    return pl.pallas_call(
        functools.partial(_mm_kernel, nk=nk, a_silu=a_silu, has_bias=bias is not None,
                          w_rows_are_outputs=w_rows_are_outputs),
        grid=(m // tm, n // tn, nk),
        in_specs=in_specs,
        out_specs=pl.BlockSpec((tm, tn), lambda i, j, k: (i, j)),
        out_shape=jax.ShapeDtypeStruct((m, n), out_dtype),
        scratch_shapes=[pltpu.VMEM(acc_shape, F32)],
        compiler_params=_params(("parallel", "parallel", "arbitrary")),
        name=name,
    )(*args)


def _gate_up_kernel(a_ref, wg_ref, wu_ref, o_ref):
    a = a_ref[...]
    g = jnp.dot(a, wg_ref[...].astype(BF16), preferred_element_type=F32)
    u = jnp.dot(a, wu_ref[...].astype(BF16), preferred_element_type=F32)
    o_ref[...] = (g * jax.nn.sigmoid(g) * u).astype(o_ref.dtype)


def gate_up(a, wg, wu, *, tm, tn):
    m, kdim = a.shape
    n = wg.shape[1]
    tm = min(tm, m)
    assert m % tm == 0 and n % tn == 0
    return pl.pallas_call(
        _gate_up_kernel,
        grid=(m // tm, n // tn),
        in_specs=[pl.BlockSpec((tm, kdim), lambda i, j: (i, 0)),
                  pl.BlockSpec((kdim, tn), lambda i, j: (0, j)),
                  pl.BlockSpec((kdim, tn), lambda i, j: (0, j))],
        out_specs=pl.BlockSpec((tm, tn), lambda i, j: (i, j)),
        out_shape=jax.ShapeDtypeStruct((m, n), BF16),
        compiler_params=_params(("parallel", "parallel")),
        name="ffn_gate_up",
    )(a, wg, wu)


def _rms(x):
    return x * lax.rsqrt(jnp.mean(x * x, axis=-1, keepdims=True) + RMS_EPS)


def _norm_mod_kernel(x_ref, g_ref, sc_ref, sh_ref, h_ref):
    h = _rms(x_ref[...]) * g_ref[...] * (1.0 + sc_ref[...]) + sh_ref[...]
    h_ref[...] = h.astype(h_ref.dtype)


def _mod_spec(mod, tm, d):
    if mod.shape[0] == 1:
        return pl.BlockSpec((1, d), lambda i: (0, 0))
    return pl.BlockSpec((tm, d), lambda i: (i, 0))


def norm_mod(x, g, scale, shift, *, tm=256):
    m, d = x.shape
    tm = min(tm, m)
    return pl.pallas_call(
        _norm_mod_kernel,
        grid=(m // tm,),
        in_specs=[pl.BlockSpec((tm, d), lambda i: (i, 0)),
                  pl.BlockSpec((1, d), lambda i: (0, 0)),
                  _mod_spec(scale, tm, d), _mod_spec(shift, tm, d)],
        out_specs=pl.BlockSpec((tm, d), lambda i: (i, 0)),
        out_shape=jax.ShapeDtypeStruct((m, d), BF16),
        compiler_params=_params(("parallel",)),
        name="norm_mod",
    )(x, g.reshape(1, d), scale, shift)


def _resid_kernel(*refs, with_mod):
    if with_mod:
        x_ref, o_ref, gp_ref, gate_ref, g2_ref, sc_ref, sh_ref, y_ref, h_ref = refs
    else:
        x_ref, o_ref, gp_ref, gate_ref, y_ref = refs
    y = x_ref[...] + gate_ref[...] * (_rms(o_ref[...]) * gp_ref[...])
    y_ref[...] = y
    if with_mod:
        h = _rms(y) * g2_ref[...] * (1.0 + sc_ref[...]) + sh_ref[...]
        h_ref[...] = h.astype(h_ref.dtype)


def resid_norm(x, o, g_post, gate, mod2=None, *, tm=256):
    m, d = x.shape
    if gate.shape[0] != 1:
        tm = tm // 2
    tm = min(tm, m)
    row = pl.BlockSpec((tm, d), lambda i: (i, 0))
    vec = pl.BlockSpec((1, d), lambda i: (0, 0))
    in_specs = [row, row, vec, _mod_spec(gate, tm, d)]
    args = [x, o, g_post.reshape(1, d), gate]
    out_specs = [row]
    out_shape = [jax.ShapeDtypeStruct((m, d), F32)]
    if mod2 is not None:
        g2, sc, sh = mod2
        in_specs += [vec, _mod_spec(sc, tm, d), _mod_spec(sh, tm, d)]
        args += [g2.reshape(1, d), sc, sh]
        out_specs.append(row)
        out_shape.append(jax.ShapeDtypeStruct((m, d), BF16))
    return pl.pallas_call(
        functools.partial(_resid_kernel, with_mod=mod2 is not None),
        grid=(m // tm,),
        in_specs=in_specs, out_specs=out_specs, out_shape=out_shape,
        compiler_params=_params(("parallel",)),
        name="resid_norm",
    )(*args)


def _rope(x, c, sa, sb):
    half = ROPE_DIM // 2
    return x * c + pltpu.roll(x, half, 1) * sa + pltpu.roll(x, HEAD_DIM - half, 1) * sb


def _prep_kernel(proj_ref, lg_ref, c_ref, sa_ref, sb_ref,
                 qsb_ref, qn_ref, qr_ref,
                 sbk_ref, sbv_ref, sbk16_ref, sbv16_ref,
                 ck_ref, cv_ref,
                 sk_ref, sv_ref, sk16_ref, sv16_ref,
                 wk_ref, wv_ref, wk16_ref, wv16_ref,
                 gates_ref):
    c, sa, sb = c_ref[...], sa_ref[...], sb_ref[...]

    def head(off, h):
        return proj_ref[:, off + h * HEAD_DIM: off + (h + 1) * HEAD_DIM]

    for h in range(SB_HEADS):
        qsb_ref[h] = head(OFF_SB_Q, h).astype(BF16)
    for h in range(NSA_HEADS):
        q = head(OFF_NSA_Q, h)
        qn_ref[h] = q.astype(BF16)
        qr_ref[h] = _rope(q, c, sa, sb).astype(BF16)
    for h in range(SB_KV_HEADS):
        sl = slice(h * HEAD_DIM, (h + 1) * HEAD_DIM)
        k = head(OFF_SB_K, h)
        v = head(OFF_SB_V, h)
        sbk_ref[:, h, :] = k
        sbv_ref[:, h, :] = v
        sbk16_ref[:, sl] = k.astype(BF16)
        sbv16_ref[:, sl] = v.astype(BF16)
    for h in range(NSA_KV_HEADS):
        sl = slice(h * HEAD_DIM, (h + 1) * HEAD_DIM)
        ck_ref[:, h, :] = head(OFF_CMP_K, h)
        cv_ref[:, h, :] = head(OFF_CMP_V, h)
        ks = _rope(head(OFF_SLC_K, h), c, sa, sb)
        sk_ref[:, h, :] = ks
        sk16_ref[:, sl] = ks.astype(BF16)
        kw = _rope(head(OFF_WIN_K, h), c, sa, sb)
        wk_ref[:, h, :] = kw
        wk16_ref[:, sl] = kw.astype(BF16)
        vs = head(OFF_SLC_V, h)
        sv_ref[:, h, :] = vs
        sv16_ref[:, sl] = vs.astype(BF16)
        vw = head(OFF_WIN_V, h)
        wv_ref[:, h, :] = vw
        wv16_ref[:, sl] = vw.astype(BF16)
    gates_ref[...] = jax.nn.sigmoid(lg_ref[...])


def rope_tables(pos):
    half = ROPE_DIM // 2
    inv = ROPE_THETA ** (-jnp.arange(half, dtype=F32) / half)
    ang = pos.astype(F32)[:, None] * inv[None, :]
    cos, sin = jnp.cos(ang), jnp.sin(ang)
    n = pos.shape[0]
    rest = HEAD_DIM - ROPE_DIM
    c = jnp.concatenate([cos, cos, jnp.ones((n, rest), F32)], axis=1)
    sa = jnp.concatenate([jnp.zeros((n, half), F32), sin, jnp.zeros((n, rest), F32)], axis=1)
    sb = jnp.concatenate([-sin, jnp.zeros((n, half + rest), F32)], axis=1)
    return c, sa, sb


def prep_heads(proj, logits, pos, *, tm=256):
    m = proj.shape[0]
    tm = min(tm, m)
    c, sa, sb = rope_tables(pos)
    row = lambda w: pl.BlockSpec((tm, w), lambda i: (i, 0))
    hm = pl.BlockSpec((16, tm, HEAD_DIM), lambda i: (0, i, 0))
    nat = lambda nh: pl.BlockSpec((tm, nh, HEAD_DIM), lambda i: (i, 0, 0))
    f = lambda w, dt: jax.ShapeDtypeStruct((m, w), dt)
    fn = lambda nh: jax.ShapeDtypeStruct((m, nh, HEAD_DIM), F32)
    hshape = jax.ShapeDtypeStruct((16, m, HEAD_DIM), BF16)
    out_specs = [hm, hm, hm,
                 nat(4), nat(4), row(512), row(512),
                 nat(2), nat(2),
                 nat(2), nat(2), row(256), row(256),
                 nat(2), nat(2), row(256), row(256),
                 row(LANES)]
    out_shape = [hshape, hshape, hshape,
                 fn(4), fn(4), f(512, BF16), f(512, BF16),
                 fn(2), fn(2),
                 fn(2), fn(2), f(256, BF16), f(256, BF16),
                 fn(2), fn(2), f(256, BF16), f(256, BF16),
                 f(LANES, F32)]
    return pl.pallas_call(
        _prep_kernel,
        grid=(m // tm,),
        in_specs=[row(proj.shape[1]), row(LANES), row(LANES), row(LANES), row(LANES)],
        out_specs=out_specs, out_shape=out_shape,
        compiler_params=_params(("parallel",)),
        name="prep_heads",
    )(proj, logits, c, sa, sb)


def _cumsum_matrix():
    j = lax.broadcasted_iota(jnp.int32, (2 * LANES, 2 * LANES), 0) & (LANES - 1)
    s = lax.broadcasted_iota(jnp.int32, (2 * LANES, 2 * LANES), 1)
    return jnp.where((s >= LANES) | (j > s), 1.0, 0.0).astype(BF16)


def _split_hi_lo(x):
    hi = x.astype(BF16)
    lo = (x - hi.astype(F32)).astype(BF16)
    return jnp.concatenate([hi, lo], axis=1)


def _sb_block(z, causal, carry, um):
    sp = jnp.maximum(z, 0.0) + jnp.log1p(jnp.exp(-jnp.abs(z)))
    ls = jnp.where(causal, -sp, 0.0)
    cb = jnp.dot(_split_hi_lo(ls), um, preferred_element_type=F32)
    w = jnp.where(causal, jnp.exp(z + ls + cb[:, :LANES] + carry), 0.0)
    return w, carry + cb[:, LANES:]


def _softmax_rows(s, mask):
    s = jnp.where(mask, s, NEG)
    m = jnp.max(s, axis=-1, keepdims=True)
    p = jnp.where(mask, jnp.exp(s - m), 0.0)
    return p / jnp.maximum(jnp.sum(p, axis=-1, keepdims=True), 1e-30)


def _sb_prompt_kernel(q_ref, k_ref, v_ref, o_ref, acc_ref, carry_ref, *, tq):
    qi = pl.program_id(1)
    rows = SB_GROUP * tq
    q = q_ref[...].reshape(rows, HEAD_DIM)
    um = _cumsum_matrix()
    tk = 2 * LANES
    qpos = qi * tq + (lax.broadcasted_iota(jnp.int32, (rows, tk), 0) & (tq - 1))
    lane = lax.broadcasted_iota(jnp.int32, (rows, tk), 1)
    acc_ref[...] = jnp.zeros_like(acc_ref)
    carry_ref[...] = jnp.zeros_like(carry_ref)

    def body(state):
        kb, _ = state
        start = pl.multiple_of(kb * tk, tk)
        z = _nt(q, k_ref[pl.ds(start, tk), :]) * SCALE
        causal = (start + lane) < qpos
        sp = jnp.maximum(z, 0.0) + jnp.log1p(jnp.exp(-jnp.abs(z)))
        ls = jnp.where(causal, -sp, 0.0)
        cb = jnp.dot(_split_hi_lo(jnp.concatenate([ls[:, LANES:], ls[:, :LANES]], axis=0)), um,
                     preferred_element_type=F32)
        carry = carry_ref[...]
        newer_total = cb[:rows, LANES:]
        log_after = jnp.concatenate([cb[rows:, :LANES] + (carry + newer_total), cb[:rows, :LANES] + carry], axis=1)
        w = jnp.where(causal, jnp.exp(z + ls + log_after), 0.0)
        acc_ref[...] += jnp.dot(w.astype(BF16), v_ref[pl.ds(start, tk), :], preferred_element_type=F32)
        carry = carry + newer_total + cb[rows:, LANES:]
        carry_ref[...] = carry
        return kb - 1, jnp.max(carry)

    lax.while_loop(lambda st: (st[0] >= 0) & (st[1] > SB_DEAD), body,
                   ((qi * tq + tq - 1) // tk, jnp.float32(0.0)))
    o_ref[...] = acc_ref[...].reshape(1, SB_GROUP, tq, HEAD_DIM)


def sb_prompt(q, k16, v16, *, tq=128):
    t = q.shape[1]
    qv = q.reshape(SB_KV_HEADS, SB_GROUP, t, HEAD_DIM)
    out = pl.pallas_call(
        functools.partial(_sb_prompt_kernel, tq=tq),
        grid=(SB_KV_HEADS, t // tq),
        in_specs=[pl.BlockSpec((1, SB_GROUP, tq, HEAD_DIM), lambda h, i: (h, 0, i, 0)),
                  pl.BlockSpec((t, HEAD_DIM), lambda h, i: (0, h)),
                  pl.BlockSpec((t, HEAD_DIM), lambda h, i: (0, h))],
        out_specs=pl.BlockSpec((1, SB_GROUP, tq, HEAD_DIM), lambda h, i: (h, 0, i, 0)),
        out_shape=jax.ShapeDtypeStruct((SB_KV_HEADS, SB_GROUP, t, HEAD_DIM), F32),
        scratch_shapes=[pltpu.VMEM((SB_GROUP * tq, HEAD_DIM), F32),
                        pltpu.VMEM((SB_GROUP * tq, LANES), F32)],
        compiler_params=_params(("parallel", "arbitrary")),
        name="sb_prompt",
    )(qv, k16, v16)
    return out.reshape(SB_HEADS, t, HEAD_DIM)


def _compress(get_chunk_rows, w1_ref, b_ref, w2_ref, n_chunks):
    rows = NSA_KV_HEADS * n_chunks
    ab = jnp.zeros((rows, 2 * CMP_HIDDEN), F32)
    for jj in range(CMP_STRIDE // 2):
        x = jnp.concatenate(
            [jnp.concatenate([get_chunk_rows(h, 2 * jj), get_chunk_rows(h, 2 * jj + 1)], axis=1)
             for h in range(NSA_KV_HEADS)], axis=0)
        ab = ab + jnp.dot(x.astype(BF16), w1_ref[jj], preferred_element_type=F32)
    first = ab[:, :CMP_HIDDEN]
    second = ab[:, CMP_HIDDEN:]
    nxt = jnp.concatenate([second[1:], second[:1]], axis=0)
    hidden = first + nxt + b_ref[...]
    return jnp.dot(jax.nn.gelu(hidden).astype(BF16), w2_ref[...], preferred_element_type=F32)


def _cmp_weights(w1, w2):
    half = CMP_STRIDE * HEAD_DIM
    a = w1[:half].reshape(CMP_STRIDE // 2, 2 * HEAD_DIM, CMP_HIDDEN)
    b = w1[half:].reshape(CMP_STRIDE // 2, 2 * HEAD_DIM, CMP_HIDDEN)
    return jnp.concatenate([a, b], axis=-1).astype(BF16), w2.astype(BF16)


def _pe_bias(pe, w1):
    row = jnp.zeros((8, pe.size), F32).at[0].set(pe.reshape(-1))
    return matmul(row, w1, tm=8, tn=CMP_HIDDEN, name="cmp_pe_bias")[:1]


def _compress_prompt_kernel(x_ref, w1_ref, b_ref, w2_ref, o_ref, *, n_chunks):
    get = lambda h, j: x_ref[pl.ds(j, n_chunks, stride=CMP_STRIDE), h, :]
    out = _compress(get, w1_ref.at[0], b_ref.at[0], w2_ref.at[0], n_chunks)
    o_ref[0] = out.reshape(NSA_KV_HEADS, n_chunks, HEAD_DIM).astype(o_ref.dtype)


def compress_prompt(ck, cv, w1s, bs, w2s):
    t = ck.shape[0]
    n_chunks = t // CMP_STRIDE
    x = jnp.stack([ck, cv])
    return pl.pallas_call(
        functools.partial(_compress_prompt_kernel, n_chunks=n_chunks),
        grid=(2,),
        in_specs=[pl.BlockSpec((None, t, NSA_KV_HEADS, HEAD_DIM), lambda a: (a, 0, 0, 0)),
                  pl.BlockSpec((1, CMP_STRIDE // 2, 2 * HEAD_DIM, 2 * CMP_HIDDEN), lambda a: (a, 0, 0, 0)),
                  pl.BlockSpec((1, 1, CMP_HIDDEN), lambda a: (a, 0, 0)),
                  pl.BlockSpec((1, CMP_HIDDEN, HEAD_DIM), lambda a: (a, 0, 0))],
        out_specs=pl.BlockSpec((1, NSA_KV_HEADS, n_chunks, HEAD_DIM), lambda a: (a, 0, 0, 0)),
        out_shape=jax.ShapeDtypeStruct((2, NSA_KV_HEADS, n_chunks, HEAD_DIM), BF16),
        compiler_params=_params(("parallel",)),
        name="compress_prompt",
    )(x, w1s, bs, w2s)


def _overlap_t(n_cmp_pad):
    n = lax.broadcasted_iota(jnp.int32, (LANES, n_cmp_pad), 0)
    c0 = lax.broadcasted_iota(jnp.int32, (LANES, n_cmp_pad), 1) * CMP_STRIDE
    s0 = n * SLC_BLOCK
    return jnp.where((c0 < s0 + SLC_BLOCK) & (c0 + CMP_BLOCK > s0), 1.0, 0.0).astype(BF16)


def _select_bias_t(imp_t, qpos_row, score_ref, n_live):
    shape = imp_t.shape
    blk = lax.broadcasted_iota(jnp.int32, shape, 0)
    cur = qpos_row >> SLC_SHIFT
    valid = blk <= cur
    forced = valid & ((blk == 0) | (blk > cur - N_LOCAL_BLOCKS))
    score = jnp.where(valid, imp_t + jnp.where(forced, FORCE_BONUS, 0.0), -1.0)
    score_ref[...] = score

    def body(m, rank):
        sm = score_ref[pl.ds(m, 1), :]
        return rank + jnp.where(blk > m, jnp.where(sm >= score, 1.0, 0.0), jnp.where(sm > score, 1.0, 0.0))

    rank = lax.fori_loop(0, n_live, body, jnp.zeros(shape, F32))
    return jnp.where(valid & (rank < SLC_TOPK), 0.0, MASK_BIAS)


def _cmp_prompt_kernel(q_ref, kc_ref, vc_ref, o_ref, bias_ref, score_ref, *, tq, n_cmp):
    qi = pl.program_id(1)
    rows = NSA_GROUP * tq
    q = q_ref[...].reshape(rows, HEAD_DIM)
    kc = kc_ref[0, 0]
    vc = vc_ref[0, 0]
    n_pad = kc.shape[0]
    s = _nt(q, kc) * SCALE
    qpos = qi * tq + (lax.broadcasted_iota(jnp.int32, (rows, n_pad), 0) & (tq - 1))
    c = lax.broadcasted_iota(jnp.int32, (rows, n_pad), 1)
    mask = (c * CMP_STRIDE + (CMP_BLOCK - 1) <= qpos) & (c < n_cmp)
    p = _softmax_rows(s, mask)
    o = jnp.dot(p.astype(BF16), vc, preferred_element_type=F32)
    o_ref[...] = o.reshape(1, NSA_GROUP, tq, HEAD_DIM)
    psum = p[:tq]
    for g in range(1, NSA_GROUP):
        psum = psum + p[g * tq:(g + 1) * tq]
    hi = psum.astype(BF16)
    lo = (psum - hi.astype(F32)).astype(BF16)
    ov = _overlap_t(n_pad)
    imp_t = _nt(ov, hi) + _nt(ov, lo)
    qpos_row = qi * tq + lax.broadcasted_iota(jnp.int32, (1, tq), 1)
    n_live = jnp.minimum((qi * tq + tq - 1) // SLC_BLOCK + 1, LANES)
    bias_ref[0] = _select_bias_t(imp_t, qpos_row, score_ref, n_live)


def cmp_prompt(q, kvc, *, tq=128):
    t = q.shape[1]
    n_chunks = kvc.shape[2]
    qv = q.reshape(NSA_KV_HEADS, NSA_GROUP, t, HEAD_DIM)
    o, bias_t = pl.pallas_call(
        functools.partial(_cmp_prompt_kernel, tq=tq, n_cmp=n_chunks - 1),
        grid=(NSA_KV_HEADS, t // tq),
        in_specs=[pl.BlockSpec((1, NSA_GROUP, tq, HEAD_DIM), lambda h, i: (h, 0, i, 0)),
                  pl.BlockSpec((1, 1, n_chunks, HEAD_DIM), lambda h, i: (0, h, 0, 0)),
                  pl.BlockSpec((1, 1, n_chunks, HEAD_DIM), lambda h, i: (1, h, 0, 0))],
        out_specs=[pl.BlockSpec((1, NSA_GROUP, tq, HEAD_DIM), lambda h, i: (h, 0, i, 0)),
                   pl.BlockSpec((1, LANES, tq), lambda h, i: (h, 0, i))],
        out_shape=[jax.ShapeDtypeStruct((NSA_KV_HEADS, NSA_GROUP, t, HEAD_DIM), F32),
                   jax.ShapeDtypeStruct((NSA_KV_HEADS, LANES, t), F32)],
        scratch_shapes=[pltpu.VMEM((LANES, tq), F32)],
        compiler_params=_params(("parallel", "parallel")),
        name="cmp_prompt",
    )(qv, kvc, kvc)
    return o.reshape(NSA_HEADS, t, HEAD_DIM), bias_t


def _slc_prompt_kernel(q_ref, bias_ref, k_ref, v_ref, o_ref, m_ref, l_ref, acc_ref, *, tq, tk):
    qi = pl.program_id(1)
    rows = NSA_GROUP * tq
    reps = tk // LANES
    q = q_ref[...].reshape(rows, HEAD_DIM)
    bias = bias_ref[0]
    qa = jnp.concatenate([q, jnp.concatenate([bias] * NSA_GROUP, axis=0)], axis=1)
    blk_row = lax.broadcasted_iota(jnp.int32, (tk, LANES), 0) >> SLC_SHIFT
    blk_col = lax.broadcasted_iota(jnp.int32, (tk, LANES), 1)
    diag = (qi * tq) // tk

    def scores(kb):
        start = pl.multiple_of(kb * tk, tk)
        onehot = jnp.where(blk_row + kb * (tk // SLC_BLOCK) == blk_col, 1.0, 0.0).astype(BF16)
        ka = jnp.concatenate([k_ref[pl.ds(start, tk), :], onehot], axis=1)
        return _nt(qa, ka), v_ref[pl.ds(start, tk), :]

    s, v = scores(diag)
    qpos = qi * tq + (lax.broadcasted_iota(jnp.int32, (rows, tk), 0) & (tq - 1))
    kpos = diag * tk + lax.broadcasted_iota(jnp.int32, (rows, tk), 1)
    s = jnp.where(kpos <= qpos, s, NEG)
    m0 = jnp.broadcast_to(jnp.max(s, axis=1, keepdims=True), (rows, LANES))
    p = jnp.exp((s - jnp.concatenate([m0] * reps, axis=1)) * SCALE)
    m_ref[...] = m0
    l_ref[...] = jnp.broadcast_to(jnp.sum(p, axis=1, keepdims=True), (rows, LANES))
    acc_ref[...] = jnp.dot(p.astype(BF16), v, preferred_element_type=F32)

    def body(i, carry):
        s, v = scores(diag - 1 - i)
        m_old = m_ref[...]
        m_new = jnp.maximum(m_old, jnp.max(s, axis=1, keepdims=True))
        alpha = jnp.exp((m_old - m_new) * SCALE)
        p = jnp.exp((s - jnp.concatenate([m_new] * reps, axis=1)) * SCALE)
        l_ref[...] = alpha * l_ref[...] + jnp.sum(p, axis=1, keepdims=True)
        acc_ref[...] = alpha * acc_ref[...] + jnp.dot(p.astype(BF16), v, preferred_element_type=F32)
        m_ref[...] = m_new
        return carry

    lax.fori_loop(0, diag, body, 0)
    o = acc_ref[...] / l_ref[...]
    o_ref[...] = o.reshape(1, NSA_GROUP, tq, HEAD_DIM)


def slc_prompt(q, bias, k16, v16, *, tq=128, tk=512):
    t = q.shape[1]
    qv = q.reshape(NSA_KV_HEADS, NSA_GROUP, t, HEAD_DIM)
    rows = NSA_GROUP * tq
    out = pl.pallas_call(
        functools.partial(_slc_prompt_kernel, tq=tq, tk=tk),
        grid=(NSA_KV_HEADS, t // tq),
        in_specs=[pl.BlockSpec((1, NSA_GROUP, tq, HEAD_DIM), lambda h, i: (h, 0, i, 0)),
                  pl.BlockSpec((1, tq, LANES), lambda h, i: (h, i, 0)),
                  pl.BlockSpec((t, HEAD_DIM), lambda h, i: (0, h)),
                  pl.BlockSpec((t, HEAD_DIM), lambda h, i: (0, h))],
        out_specs=pl.BlockSpec((1, NSA_GROUP, tq, HEAD_DIM), lambda h, i: (h, 0, i, 0)),
        out_shape=jax.ShapeDtypeStruct((NSA_KV_HEADS, NSA_GROUP, t, HEAD_DIM), F32),
        scratch_shapes=[pltpu.VMEM((rows, LANES), F32), pltpu.VMEM((rows, LANES), F32),
                        pltpu.VMEM((rows, HEAD_DIM), F32)],
        compiler_params=_params(("parallel", "arbitrary")),
        name="slc_prompt",
    )(qv, bias, k16, v16)
    return out.reshape(NSA_HEADS, t, HEAD_DIM)


def _win_prompt_kernel(q_ref, k_ref, v_ref, o_ref, *, tq):
    qi = pl.program_id(1)
    rows = NSA_GROUP * tq
    span = WINDOW + tq
    q = q_ref[...].reshape(rows, HEAD_DIM)
    start = pl.multiple_of(jnp.maximum(qi * tq - WINDOW, 0), tq)
    s = _nt(q, k_ref[pl.ds(start, span), :])
    qpos = qi * tq + (lax.broadcasted_iota(jnp.int32, (rows, span), 0) & (tq - 1))
    dist = qpos - (start + lax.broadcasted_iota(jnp.int32, (rows, span), 1))
    s = jnp.where((dist >= 0) & (dist < WINDOW), s, NEG)
    p = jnp.exp((s - jnp.max(s, axis=1, keepdims=True)) * SCALE)
    o = jnp.dot(p.astype(BF16), v_ref[pl.ds(start, span), :], preferred_element_type=F32)
    o = o / jnp.sum(p, axis=1, keepdims=True)
    o_ref[...] = o.reshape(1, NSA_GROUP, tq, HEAD_DIM)


def win_prompt(q, k16, v16, *, tq=128):
    t = q.shape[1]
    qv = q.reshape(NSA_KV_HEADS, NSA_GROUP, t, HEAD_DIM)
    rows = NSA_GROUP * tq
    out = pl.pallas_call(
        functools.partial(_win_prompt_kernel, tq=tq),
        grid=(NSA_KV_HEADS, t // tq),
        in_specs=[pl.BlockSpec((1, NSA_GROUP, tq, HEAD_DIM), lambda h, i: (h, 0, i, 0)),
                  pl.BlockSpec((t, HEAD_DIM), lambda h, i: (0, h)),
                  pl.BlockSpec((t, HEAD_DIM), lambda h, i: (0, h))],
        out_specs=pl.BlockSpec((1, NSA_GROUP, tq, HEAD_DIM), lambda h, i: (h, 0, i, 0)),
        out_shape=jax.ShapeDtypeStruct((NSA_KV_HEADS, NSA_GROUP, t, HEAD_DIM), F32),
        compiler_params=_params(("parallel", "arbitrary")),
        name="win_prompt",
    )(qv, k16, v16)
    return out.reshape(NSA_HEADS, t, HEAD_DIM)


def _combine_kernel(osb_ref, oc_ref, os_ref, ow_ref, gates_ref, gsb_ref, gnsa_ref, out_ref):
    gsb = gsb_ref[...]
    gnsa = gnsa_ref[...]
    gates = gates_ref[...]
    for h in range(SB_HEADS):
        out_ref[:, h * HEAD_DIM:(h + 1) * HEAD_DIM] = (_rms(osb_ref[h]) * gsb).astype(out_ref.dtype)
    for h in range(NSA_HEADS):
        o = (gates[:, h:h + 1] * oc_ref[h]
             + gates[:, NSA_HEADS + h:NSA_HEADS + h + 1] * os_ref[h]
             + gates[:, 2 * NSA_HEADS + h:2 * NSA_HEADS + h + 1] * ow_ref[h])
        col = (SB_HEADS + h) * HEAD_DIM
        out_ref[:, col:col + HEAD_DIM] = (_rms(o) * gnsa).astype(out_ref.dtype)


def combine_heads(o_sb, o_cmp, o_slc, o_win, gates, g_sb, g_nsa, *, tm=256):
    m = o_sb.shape[1]
    tm = min(tm, m)
    hm = pl.BlockSpec((16, tm, HEAD_DIM), lambda i: (0, i, 0))
    vec = pl.BlockSpec((1, HEAD_DIM), lambda i: (0, 0))
    width = (SB_HEADS + NSA_HEADS) * HEAD_DIM
    return pl.pallas_call(
        _combine_kernel,
        grid=(m // tm,),
        in_specs=[hm, hm, hm, hm, pl.BlockSpec((tm, LANES), lambda i: (i, 0)), vec, vec],
        out_specs=pl.BlockSpec((tm, width), lambda i: (i, 0)),
        out_shape=jax.ShapeDtypeStruct((m, width), BF16),
        compiler_params=_params(("parallel",)),
        name="combine_heads",
    )(o_sb, o_cmp, o_slc, o_win, gates, g_sb.reshape(1, -1), g_nsa.reshape(1, -1))


def _pool_rows(pool):
    d, n_pool, page, nh, hd = pool.shape
    return pool.reshape(d * n_pool, page * nh, hd)


def _page_specs(n_pages, n_heads, page_offset):
    return [pl.BlockSpec((None, PAGE * n_heads, HEAD_DIM),
                         functools.partial(lambda p, s, pt: (page_offset + pt[s, p], 0, 0), p))
            for p in range(n_pages)]


def _head_rows(ref, h, n_heads, n_rows=PAGE, first=0, every=1):
    return ref[pl.ds(first * n_heads + h, n_rows, stride=every * n_heads), :]


def _gather_head(page_refs, new_ref, h, n_heads):
    parts = [_head_rows(r, h, n_heads).astype(BF16) for r in page_refs]
    parts.append(new_ref[0, :, h * HEAD_DIM:(h + 1) * HEAD_DIM])
    return jnp.concatenate(parts, axis=0)


def _sb_sample_body(q_ref, kn_ref, vn_ref, k_pages, v_pages, o_ref, *, n_pages, n_new):
    past = n_pages * PAGE
    nb = n_pages + 1
    rows = n_new * SB_GROUP
    grp = SB_KV_HEADS * rows
    um = _cumsum_matrix()
    zs = [_nt(q_ref[0, h], _gather_head(k_pages, kn_ref, h, SB_KV_HEADS)) for h in range(SB_KV_HEADS)]
    z = jnp.concatenate([zh[:, b * PAGE:(b + 1) * PAGE] for b in range(nb) for zh in zs], axis=0) * SCALE
    i = lax.broadcasted_iota(jnp.int32, (nb * grp, LANES), 0)
    assert grp & (grp - 1) == 0, "new tokens per sequence must be a power of two"
    kpos = (i >> (grp.bit_length() - 1)) * PAGE + lax.broadcasted_iota(jnp.int32, (nb * grp, LANES), 1)
    qpos = past + ((i & (rows - 1)) >> SB_GROUP_SHIFT)
    causal = kpos < qpos
    sp = jnp.maximum(z, 0.0) + jnp.log1p(jnp.exp(-jnp.abs(z)))
    ls = jnp.where(causal, -sp, 0.0)
    cb = jnp.dot(_split_hi_lo(ls), um, preferred_element_type=F32)
    carries = [None] * nb
    carry = jnp.zeros((grp, LANES), F32)
    for b in range(nb - 1, -1, -1):
        carries[b] = carry
        carry = carry + cb[b * grp:(b + 1) * grp, LANES:]
    w = jnp.where(causal, jnp.exp(z + ls + cb[:, :LANES] + jnp.concatenate(carries, axis=0)), 0.0)
    for h in range(SB_KV_HEADS):
        wh = jnp.concatenate([w[b * grp + h * rows:b * grp + (h + 1) * rows] for b in range(nb)], axis=1)
        vh = _gather_head(v_pages, vn_ref, h, SB_KV_HEADS)
        o_ref[0, h] = jnp.dot(wh.astype(BF16), vh, preferred_element_type=F32)


def _cmp_sample_body(q_ref, kn_ref, vn_ref, w1_ref, b_ref, w2_ref, k_pages, v_pages, o_ref, *, n_pages, n_new):
    per_page = PAGE // CMP_STRIDE
    past = n_pages * PAGE
    n_chunks = n_pages * per_page + 8
    n_cmp = n_chunks - 8
    rows = n_new * NSA_GROUP

    def summaries(pages, new_ref, which):
        tok_major = [[jnp.swapaxes(_head_rows(r, h, NSA_KV_HEADS).reshape(per_page, CMP_STRIDE, HEAD_DIM), 0, 1)
                      for r in pages] for h in range(NSA_KV_HEADS)]

        def get(h, j):
            parts = [y[j] for y in tok_major[h]]
            parts.append(new_ref[0, j][:, h * HEAD_DIM:(h + 1) * HEAD_DIM])
            return jnp.concatenate(parts, axis=0)

        return _compress(get, w1_ref.at[which], b_ref.at[which], w2_ref.at[which], n_chunks)

    kc_all = summaries(k_pages, kn_ref, 0)
    vc_all = summaries(v_pages, vn_ref, 1)

    c = lax.broadcasted_iota(jnp.int32, (rows, n_chunks), 1)
    t_row = lax.broadcasted_iota(jnp.int32, (rows, n_chunks), 0) >> NSA_GROUP_SHIFT
    mask = (c * CMP_STRIDE + (CMP_BLOCK - 1) <= past + t_row) & (c < n_cmp)
    cc = lax.broadcasted_iota(jnp.int32, (n_chunks, LANES), 0) * CMP_STRIDE
    s0 = lax.broadcasted_iota(jnp.int32, (n_chunks, LANES), 1) * SLC_BLOCK
    ov = jnp.where((cc < s0 + SLC_BLOCK) & (cc + CMP_BLOCK > s0), 1.0, 0.0).astype(BF16)
    blk = lax.broadcasted_iota(jnp.int32, (n_new, LANES), 1)
    cur = (past + lax.broadcasted_iota(jnp.int32, (n_new, LANES), 0)) >> SLC_SHIFT
    valid = blk <= cur
    forced = valid & ((blk == 0) | (blk > cur - N_LOCAL_BLOCKS))
    n_slc = past // SLC_BLOCK + -(-n_new // SLC_BLOCK)
    biases = []
    for h in range(NSA_KV_HEADS):
        kc = kc_all[h * n_chunks:(h + 1) * n_chunks].astype(BF16)
        vc = vc_all[h * n_chunks:(h + 1) * n_chunks].astype(BF16)
        p = _softmax_rows(_nt(q_ref[0, h], kc) * SCALE, mask)
        o_ref[0, h] = jnp.dot(p.astype(BF16), vc, preferred_element_type=F32)
        psum = jnp.sum(p.reshape(n_new, NSA_GROUP, n_chunks), axis=1)
        hi = psum.astype(BF16)
        lo = (psum - hi.astype(F32)).astype(BF16)
        imp = jnp.dot(hi, ov, preferred_element_type=F32) + jnp.dot(lo, ov, preferred_element_type=F32)
        score = jnp.where(valid, imp + jnp.where(forced, FORCE_BONUS, 0.0), -1.0)
        rank = jnp.zeros((n_new, LANES), F32)
        for m in range(n_slc):
            sm = score[:, m:m + 1]
            rank = rank + jnp.where(blk > m, jnp.where(sm >= score, 1.0, 0.0), jnp.where(sm > score, 1.0, 0.0))
        biases.append(jnp.where(valid & (rank < SLC_TOPK), 0.0, MASK_BIAS))
    return biases


def _slc_sample_body(q_ref, biases, kn_ref, vn_ref, k_pages, v_pages, o_ref, *, n_pages, n_new):
    past = n_pages * PAGE
    nk = (n_pages + 1) * PAGE
    rows = n_new * NSA_GROUP
    kpos = lax.broadcasted_iota(jnp.int32, (rows, nk), 1)
    qpos = past + (lax.broadcasted_iota(jnp.int32, (rows, nk), 0) >> NSA_GROUP_SHIFT)
    blk_of_key = lax.broadcasted_iota(jnp.int32, (LANES, nk), 1) >> SLC_SHIFT
    expand = jnp.where(blk_of_key == lax.broadcasted_iota(jnp.int32, (LANES, nk), 0), 1.0, 0.0).astype(BF16)
    for h in range(NSA_KV_HEADS):
        kh = _gather_head(k_pages, kn_ref, h, NSA_KV_HEADS)
        vh = _gather_head(v_pages, vn_ref, h, NSA_KV_HEADS)
        bias = jnp.concatenate([jnp.broadcast_to(biases[h][t:t + 1], (NSA_GROUP, LANES)) for t in range(n_new)],
                               axis=0).astype(BF16)
        s = (_nt(q_ref[0, h], kh) + jnp.dot(bias, expand, preferred_element_type=F32)) * SCALE
        mask = (kpos <= qpos) & (s > 0.5 * MASK_BIAS * SCALE)
        p = _softmax_rows(s, mask)
        o_ref[0, h] = jnp.dot(p.astype(BF16), vh, preferred_element_type=F32)


def _win_sample_body(q_ref, kn_ref, vn_ref, kb_ref, vb_ref, o_ref, ko_ref, vo_ref, *, n_new, past):
    nh = NSA_KV_HEADS
    win = kb_ref.shape[0] // nh
    rows = n_new * NSA_GROUP
    nk = win + 8
    kpos = past - win + lax.broadcasted_iota(jnp.int32, (rows, nk), 1)
    qpos = past + (lax.broadcasted_iota(jnp.int32, (rows, nk), 0) >> NSA_GROUP_SHIFT)
    dist = qpos - kpos
    mask = (dist >= 0) & (dist < WINDOW) & (kpos >= 0)
    for h in range(nh):
        sl = slice(h * HEAD_DIM, (h + 1) * HEAD_DIM)
        kall = jnp.concatenate([_head_rows(kb_ref, h, nh, win), kn_ref[0][:, sl]], axis=0)
        vall = jnp.concatenate([_head_rows(vb_ref, h, nh, win), vn_ref[0][:, sl]], axis=0)
        ko_ref[pl.ds(h, win, stride=nh), :] = kall[n_new:n_new + win]
        vo_ref[pl.ds(h, win, stride=nh), :] = vall[n_new:n_new + win]
        p = _softmax_rows(_nt(q_ref[0, h], kall.astype(BF16)) * SCALE, mask)
        o_ref[0, h] = jnp.dot(p.astype(BF16), vall.astype(BF16), preferred_element_type=F32)


def _sample_mixers_kernel(pt_ref, q_sb_ref, q_nsa_ref, q_rot_ref,
                          sbk_new_ref, sbv_new_ref, ck_new_ref, cv_new_ref, sk_new_ref, sv_new_ref,
                          wk_new_ref, wv_new_ref, w1_ref, b_ref, w2_ref, wk_buf_ref, wv_buf_ref,
                          *refs, n_pages, n_new):
    pools = [refs[i * n_pages:(i + 1) * n_pages] for i in range(6)]
    o_sb_ref, o_cmp_ref, o_slc_ref, o_win_ref, wk_out_ref, wv_out_ref = refs[6 * n_pages:]
    sizes = dict(n_pages=n_pages, n_new=n_new)
    _sb_sample_body(q_sb_ref, sbk_new_ref, sbv_new_ref, pools[0], pools[1], o_sb_ref, **sizes)
    biases = _cmp_sample_body(q_nsa_ref, ck_new_ref, cv_new_ref, w1_ref, b_ref, w2_ref, pools[2], pools[3],
                              o_cmp_ref, **sizes)
    _slc_sample_body(q_rot_ref, biases, sk_new_ref, sv_new_ref, pools[4], pools[5], o_slc_ref, **sizes)
    _win_sample_body(q_rot_ref, wk_new_ref, wv_new_ref, wk_buf_ref, wv_buf_ref, o_win_ref, wk_out_ref, wv_out_ref,
                     n_new=n_new, past=n_pages * PAGE)


def sample_mixers(page_table, q_sb, q_nsa, q_rot, new_rows, cmp_w, caches, layer):
    b, n_pages = page_table.shape
    sb_rows, nsa_rows = q_sb.shape[2], q_nsa.shape[2]
    n_new = nsa_rows // NSA_GROUP
    sbk_new, sbv_new, ck_new, cv_new, sk_new, sv_new, wk_new, wv_new = new_rows
    sb_k_pool, sb_v_pool, cmp_k_pool, cmp_v_pool, slc_k_pool, slc_v_pool, wk_buf, wv_buf = caches
    _, _, win, nh, _ = wk_buf.shape
    per_seq = lambda a: pl.BlockSpec((1,) + a.shape[1:], lambda s, pt: (s,) + (0,) * (a.ndim - 1))
    full = lambda a: pl.BlockSpec(a.shape, lambda s, pt: (0,) * a.ndim)
    buf_in = pl.BlockSpec((None, win * nh, HEAD_DIM), lambda s, pt: (layer * b + s, 0, 0))
    buf_out = pl.BlockSpec((None, win * nh, HEAD_DIM), lambda s, pt: (s, 0, 0))
    buf_shape = jax.ShapeDtypeStruct((b, win * nh, HEAD_DIM), F32)
    sb_out = jax.ShapeDtypeStruct((b, SB_KV_HEADS, sb_rows, HEAD_DIM), F32)
    nsa_out = jax.ShapeDtypeStruct((b, NSA_KV_HEADS, nsa_rows, HEAD_DIM), F32)
    pools = (sb_k_pool, sb_v_pool, cmp_k_pool, cmp_v_pool, slc_k_pool, slc_v_pool)
    page_specs, page_args = [], []
    for pool in pools:
        page_specs += _page_specs(n_pages, pool.shape[3], layer * pool.shape[1])
        page_args += [_pool_rows(pool)] * n_pages
    singles = (q_sb, q_nsa, q_rot, sbk_new, sbv_new, ck_new, cv_new, sk_new, sv_new, wk_new, wv_new)
    grid_spec = pltpu.PrefetchScalarGridSpec(
        num_scalar_prefetch=1, grid=(b,),
        in_specs=[per_seq(a) for a in singles] + [full(a) for a in cmp_w] + [buf_in, buf_in] + page_specs,
        out_specs=[per_seq(sb_out), per_seq(nsa_out), per_seq(nsa_out), per_seq(nsa_out), buf_out, buf_out])
    o_sb, o_cmp, o_slc, o_win, k_out, v_out = pl.pallas_call(
        functools.partial(_sample_mixers_kernel, n_pages=n_pages, n_new=n_new),
        grid_spec=grid_spec,
        out_shape=[sb_out, nsa_out, nsa_out, nsa_out, buf_shape, buf_shape],
        compiler_params=_params(("parallel",)),
        name="sample_mixers",
    )(page_table, *singles, *cmp_w, _pool_rows(wk_buf), _pool_rows(wv_buf), *page_args)
    return o_sb, o_cmp, o_slc, o_win, k_out.reshape(b, win, nh, HEAD_DIM), v_out.reshape(b, win, nh, HEAD_DIM)


def _project(h, w_in):
    w_t = jnp.swapaxes(w_in, 0, 1)
    proj = matmul(h, w_t, tm=1024, tn=512, n_cols=OFF_GATE, w_rows_are_outputs=True, name="in_proj")
    w_gate_logits = jnp.pad(w_t[OFF_GATE:], ((0, LANES - N_GATE), (0, 0)))
    logits = matmul(h, w_gate_logits, tm=1024, tn=LANES, w_rows_are_outputs=True, name="gate_logits")
    return proj, logits


def _finish_layer(x, mixed, mod, w):
    shift_m, scale_m, gate_m, shift_f, scale_f, gate_f = mod
    o = matmul(mixed, w['w_o'], tm=1024, tn=512, name="out_proj")
    x1, h2 = resid_norm(x, o, w['g_post_mix'], gate_m, (w['g_pre_ffn'], scale_f, shift_f))
    act = gate_up(h2, w['w_gate'], w['w_up'], tm=1024, tn=256)
    f = matmul(act, w['w_down'], tm=1024, tn=256, a_single_buffer=True, name="ffn_down")
    return resid_norm(x1, f, w['g_post_ffn'], gate_f)[0]


def _prompt_layer(x, mod, w, cmp_w):
    t = x.shape[0]
    shift_m, scale_m = mod[0], mod[1]
    h = norm_mod(x, w['g_pre_mix'], scale_m, shift_m)
    proj, logits = _project(h, w['w_in'])
    (q_sb, q_nsa, q_rot, sbk, sbv, sbk16, sbv16, ck, cv, sk, sv, sk16, sv16,
     wk, wv, wk16, wv16, gates) = prep_heads(proj, logits, jnp.arange(t))
    o_sb = sb_prompt(q_sb, sbk16, sbv16)
    kvc = compress_prompt(ck, cv, *cmp_w)
    o_cmp, bias_t = cmp_prompt(q_nsa, kvc)
    bias = jnp.swapaxes(bias_t, 1, 2).astype(BF16)
    o_slc = slc_prompt(q_rot, bias, sk16, sv16)
    o_win = win_prompt(q_rot, wk16, wv16)
    mixed = combine_heads(o_sb, o_cmp, o_slc, o_win, gates, w['g_out_sb'], w['g_out_nsa'])
    y = _finish_layer(x, mixed, mod, w)
    win = min(WINDOW, t)
    states = tuple(a[None] for a in (sbk, sbv, ck, cv, sk, sv, wk[t - win:], wv[t - win:]))
    return y, states


def _seq_rows(a, b, n_new, group):
    kvh = a.shape[0] // group
    a = a.reshape(kvh, group, b, n_new, HEAD_DIM)
    return a.transpose(2, 0, 3, 1, 4).reshape(b, kvh, n_new * group, HEAD_DIM)


def _head_major(o, b, n_new, group):
    kvh = o.shape[1]
    o = o.reshape(b, kvh, n_new, group, HEAD_DIM)
    return o.transpose(1, 3, 0, 2, 4).reshape(kvh * group, b * n_new, HEAD_DIM)


def _pad_rows(a, b, n_new, n_rows):
    a = a.reshape(b, n_new, -1)
    return jnp.pad(a, ((0, 0), (0, n_rows - n_new), (0, 0)))


def _sample_layer(x, mod, w, cmp_w, caches, page_table, layer):
    b, n_pages = page_table.shape
    m = x.shape[0]
    n_new = m // b
    past = n_pages * PAGE
    sb_k_pool, sb_v_pool, cmp_k_pool, cmp_v_pool, slc_k_pool, slc_v_pool, win_k_buf, win_v_buf = caches
    shift_m, scale_m = mod[0], mod[1]
    h = norm_mod(x, w['g_pre_mix'], scale_m, shift_m)
    proj, logits = _project(h, w['w_in'])
    pos = jnp.tile(past + jnp.arange(n_new), b)
    (q_sb, q_nsa, q_rot, sbk, sbv, sbk16, sbv16, ck, cv, sk, sv, sk16, sv16,
     wk, wv, wk16, wv16, gates) = prep_heads(proj, logits, pos)

    chunk_new = lambda a: jnp.pad(_pad_rows(a, b, n_new, CMP_STRIDE)[:, :, None, :],
                                  ((0, 0), (0, 0), (0, 7), (0, 0)))
    new_rows = (_pad_rows(sbk16, b, n_new, PAGE), _pad_rows(sbv16, b, n_new, PAGE),
                chunk_new(ck), chunk_new(cv),
                _pad_rows(sk16, b, n_new, PAGE), _pad_rows(sv16, b, n_new, PAGE),
                _pad_rows(wk, b, n_new, 8), _pad_rows(wv, b, n_new, 8))
    o_sb, o_cmp, o_slc, o_win, win_k_new, win_v_new = sample_mixers(
        page_table, _seq_rows(q_sb, b, n_new, SB_GROUP), _seq_rows(q_nsa, b, n_new, NSA_GROUP),
        _seq_rows(q_rot, b, n_new, NSA_GROUP), new_rows, cmp_w, caches, layer)
    mixed = combine_heads(_head_major(o_sb, b, n_new, SB_GROUP), _head_major(o_cmp, b, n_new, NSA_GROUP),
                          _head_major(o_slc, b, n_new, NSA_GROUP), _head_major(o_win, b, n_new, NSA_GROUP),
                          gates, w['g_out_sb'], w['g_out_nsa'])
    y = _finish_layer(x, mixed, mod, w)
    st = lambda a: a.reshape(b, n_new, *a.shape[1:])
    states = (st(sbk), st(sbv), st(ck), st(cv), st(sk), st(sv), win_k_new, win_v_new)
    return y, states


def kernel(x_prompt, x_sample, cache_sb_k, cache_sb_v, cache_cmp_k, cache_cmp_v, cache_slc_k, cache_slc_v, cache_win_k, cache_win_v, page_table, c_prompt, c_sample, w_ada, b_ada, g_pre_mix, w_in, cmp_pe_k, w_cmp_k1, w_cmp_k2, cmp_pe_v, w_cmp_v1, w_cmp_v2, g_out_sb, g_out_nsa, w_o, g_post_mix, g_pre_ffn, w_gate, w_up, w_down, g_post_ffn):
    depth = w_ada.shape[0]
    bp, t, d = x_prompt.shape
    bs, n_new, _ = x_sample.shape
    assert bp == 1
    y_p = x_prompt.reshape(t, d)
    y_s = x_sample.reshape(bs * n_new, d)
    per_layer = []
    for l in range(depth):
        w = {'g_pre_mix': g_pre_mix[l], 'w_in': w_in[l], 'g_out_sb': g_out_sb[l], 'g_out_nsa': g_out_nsa[l],
             'w_o': w_o[l], 'g_post_mix': g_post_mix[l], 'g_pre_ffn': g_pre_ffn[l], 'w_gate': w_gate[l],
             'w_up': w_up[l], 'w_down': w_down[l], 'g_post_ffn': g_post_ffn[l]}
        n_c = bp + bs
        c_all = jnp.pad(jnp.concatenate([c_prompt, c_sample], axis=0), ((0, -n_c % 16), (0, 0)))
        mod = matmul(c_all, w_ada[l], tm=c_all.shape[0], tn=512, bias=b_ada[l], a_silu=True, name="ada_mod")
        mod_p = jnp.split(mod[:bp], N_ADA, axis=-1)
        mod_s = jnp.split(jnp.repeat(mod[bp:n_c], n_new, axis=0), N_ADA, axis=-1)
        w1k, w2k = _cmp_weights(w_cmp_k1[l], w_cmp_k2[l])
        w1v, w2v = _cmp_weights(w_cmp_v1[l], w_cmp_v2[l])
        cmp_w = (jnp.stack([w1k, w1v]),
                 jnp.stack([_pe_bias(cmp_pe_k[l], w_cmp_k1[l]), _pe_bias(cmp_pe_v[l], w_cmp_v1[l])]),
                 jnp.stack([w2k, w2v]))
        y_p, st_p = _prompt_layer(y_p, mod_p, w, cmp_w)
        caches = (cache_sb_k, cache_sb_v, cache_cmp_k, cache_cmp_v,
                  cache_slc_k, cache_slc_v, cache_win_k, cache_win_v)
        y_s, st_s = _sample_layer(y_s, mod_s, w, cmp_w, caches, page_table, l)
        per_layer.append(st_p + st_s)
    new = [jnp.stack(s) for s in zip(*per_layer)]
    return (y_p.reshape(bp, t, d), y_s.reshape(bs, n_new, d), *new)
```

```python
import functools

import numpy as np
import jax
import jax.numpy as jnp
from jax import lax
from jax.experimental import pallas as pl
from jax.experimental.pallas import tpu as pltpu

F32 = jnp.float32
BF16 = jnp.bfloat16

HEAD_DIM = 128
SB_HEADS = 16
SB_KV_HEADS = 4
SB_GROUP = SB_HEADS // SB_KV_HEADS
NSA_HEADS = 16
NSA_KV_HEADS = 2
NSA_GROUP = NSA_HEADS // NSA_KV_HEADS
ROPE_DIM = HEAD_DIM // 4
ROPE_THETA = 500000.0
CMP_STRIDE = 16
CMP_BLOCK = 32
CMP_HIDDEN = 2 * HEAD_DIM
SLC_BLOCK = 64
SLC_TOPK = 16
N_LOCAL_BLOCKS = 2
FORCE_BONUS = 1.0e4
WINDOW = 512
RMS_EPS = 1e-6
N_ADA = 6
PAGE = 128
SCALE = HEAD_DIM ** -0.5
SLC_SHIFT = 6
SB_GROUP_SHIFT = 2
NSA_GROUP_SHIFT = 3

LANES = 128
MASK_BIAS = -float(2 ** 20)
NEG = -1.0e30
SB_DEAD = -110.0
VMEM_LIMIT = 56 * 1024 * 1024

OFF_SB_Q = 0
OFF_SB_K = 2048
OFF_SB_V = 2560
OFF_NSA_Q = 3072
OFF_CMP_K = 5120
OFF_CMP_V = 5376
OFF_SLC_K = 5632
OFF_SLC_V = 5888
OFF_WIN_K = 6144
OFF_WIN_V = 6400
OFF_GATE = 6656
N_GATE = 3 * NSA_HEADS


def _params(sem, vmem=VMEM_LIMIT):
    return pltpu.CompilerParams(dimension_semantics=sem, vmem_limit_bytes=vmem)


def _nt(a, b):
    return lax.dot_general(a, b, (((1,), (1,)), ((), ())), preferred_element_type=F32)


def _mm_kernel(*refs, nk, a_silu, has_bias, w_rows_are_outputs):
    if has_bias:
        a_ref, w_ref, b_ref, o_ref, acc_ref = refs
    else:
        a_ref, w_ref, o_ref, acc_ref = refs
        b_ref = None
    a = a_ref[...]
    if a_silu:
        a = a * jax.nn.sigmoid(a)
    if w_rows_are_outputs:
        p = _nt(a.astype(BF16), w_ref[...].astype(BF16))
    else:
        p = jnp.dot(a.astype(BF16), w_ref[...].astype(BF16), preferred_element_type=F32)

    def finish(v):
        if has_bias:
            v = v + b_ref[...]
        o_ref[...] = v.astype(o_ref.dtype)

    if nk == 1:
        finish(p)
    else:
        k = pl.program_id(2)

        @pl.when(k == 0)
        def _():
            acc_ref[...] = p

        @pl.when(k > 0)
        def _():
            acc_ref[...] += p

        @pl.when(k == nk - 1)
        def _():
            finish(acc_ref[...])


def matmul(a, w, *, tm, tn, tk=None, n_cols=None, bias=None, a_silu=False, out_dtype=F32, a_single_buffer=False,
           w_rows_are_outputs=False, name="matmul"):
    m, kdim = a.shape
    n = w.shape[0 if w_rows_are_outputs else 1] if n_cols is None else n_cols
    tk = kdim if tk is None else tk
    tm = min(tm, m)
    assert m % tm == 0 and n % tn == 0 and kdim % tk == 0
    nk = kdim // tk
    a_mode = {"pipeline_mode": pl.Buffered(1)} if a_single_buffer else {}
    w_spec = (pl.BlockSpec((tn, tk), lambda i, j, k: (j, k)) if w_rows_are_outputs
              else pl.BlockSpec((tk, tn), lambda i, j, k: (k, j)))
    in_specs = [pl.BlockSpec((tm, tk), lambda i, j, k: (i, k), **a_mode), w_spec]
    args = [a, w]
    if bias is not None:
        in_specs.append(pl.BlockSpec((1, tn), lambda i, j, k: (0, j)))
        args.append(bias.reshape(1, -1))
    acc_shape = (tm, tn) if nk > 1 else (8, LANES)
    return pl.pallas_call(
        functools.partial(_mm_kernel, nk=nk, a_silu=a_silu, has_bias=bias is not None,
                          w_rows_are_outputs=w_rows_are_outputs),
        grid=(m // tm, n // tn, nk),
        in_specs=in_specs,
        out_specs=pl.BlockSpec((tm, tn), lambda i, j, k: (i, j)),
        out_shape=jax.ShapeDtypeStruct((m, n), out_dtype),
        scratch_shapes=[pltpu.VMEM(acc_shape, F32)],
        compiler_params=_params(("parallel", "parallel", "arbitrary")),
        name=name,
    )(*args)


def _gate_up_kernel(a_ref, wg_ref, wu_ref, o_ref):
    a = a_ref[...]
    g = jnp.dot(a, wg_ref[...].astype(BF16), preferred_element_type=F32)
    u = jnp.dot(a, wu_ref[...].astype(BF16), preferred_element_type=F32)
    o_ref[...] = (g * jax.nn.sigmoid(g) * u).astype(o_ref.dtype)


def gate_up(a, wg, wu, *, tm, tn):
    m, kdim = a.shape
    n = wg.shape[1]
    tm = min(tm, m)
    assert m % tm == 0 and n % tn == 0
    return pl.pallas_call(
        _gate_up_kernel,
        grid=(m // tm, n // tn),
        in_specs=[pl.BlockSpec((tm, kdim), lambda i, j: (i, 0), pipeline_mode=pl.Buffered(1)),
                  pl.BlockSpec((kdim, tn), lambda i, j: (0, j)),
                  pl.BlockSpec((kdim, tn), lambda i, j: (0, j))],
        out_specs=pl.BlockSpec((tm, tn), lambda i, j: (i, j)),
        out_shape=jax.ShapeDtypeStruct((m, n), BF16),
        compiler_params=_params(("parallel", "parallel")),
        name="ffn_gate_up",
    )(a, wg, wu)


def _rms(x):
    return x * lax.rsqrt(jnp.mean(x * x, axis=-1, keepdims=True) + RMS_EPS)


def _norm_mod_kernel(x_ref, g_ref, sc_ref, sh_ref, h_ref):
    h = _rms(x_ref[...]) * g_ref[...] * (1.0 + sc_ref[...]) + sh_ref[...]
    h_ref[...] = h.astype(h_ref.dtype)


def _mod_spec(mod, tm, d):
    if mod.shape[0] == 1:
        return pl.BlockSpec((1, d), lambda i: (0, 0))
    n_blocks = mod.shape[0] // tm
    return pl.BlockSpec((tm, d), lambda i: (i % n_blocks, 0))


def _mod_tile(mod, tm, m):
    tm = min(tm, m)
    if mod.shape[0] != 1:
        tm = min(tm, mod.shape[0])
        assert mod.shape[0] % tm == 0 and m % mod.shape[0] == 0
    return tm


def norm_mod(x, g, scale, shift, *, tm=256):
    m, d = x.shape
    tm = _mod_tile(scale, tm, m)
    return pl.pallas_call(
        _norm_mod_kernel,
        grid=(m // tm,),
        in_specs=[pl.BlockSpec((tm, d), lambda i: (i, 0)),
                  pl.BlockSpec((1, d), lambda i: (0, 0)),
                  _mod_spec(scale, tm, d), _mod_spec(shift, tm, d)],
        out_specs=pl.BlockSpec((tm, d), lambda i: (i, 0)),
        out_shape=jax.ShapeDtypeStruct((m, d), BF16),
        compiler_params=_params(("parallel",)),
        name="norm_mod",
    )(x, g.reshape(1, d), scale, shift)


def _resid_kernel(*refs, with_mod):
    if with_mod:
        x_ref, o_ref, gp_ref, gate_ref, g2_ref, sc_ref, sh_ref, y_ref, h_ref = refs
    else:
        x_ref, o_ref, gp_ref, gate_ref, y_ref = refs
    y = x_ref[...] + gate_ref[...] * (_rms(o_ref[...]) * gp_ref[...])
    y_ref[...] = y
    if with_mod:
        h = _rms(y) * g2_ref[...] * (1.0 + sc_ref[...]) + sh_ref[...]
        h_ref[...] = h.astype(h_ref.dtype)


def resid_norm(x, o, g_post, gate, mod2=None, *, tm=256):
    m, d = x.shape
    if gate.shape[0] != 1:
        tm = tm // 2
    tm = _mod_tile(gate, tm, m)
    row = pl.BlockSpec((tm, d), lambda i: (i, 0))
    vec = pl.BlockSpec((1, d), lambda i: (0, 0))
    in_specs = [row, row, vec, _mod_spec(gate, tm, d)]
    args = [x, o, g_post.reshape(1, d), gate]
    out_specs = [row]
    out_shape = [jax.ShapeDtypeStruct((m, d), F32)]
    if mod2 is not None:
        g2, sc, sh = mod2
        in_specs += [vec, _mod_spec(sc, tm, d), _mod_spec(sh, tm, d)]
        args += [g2.reshape(1, d), sc, sh]
        out_specs.append(row)
        out_shape.append(jax.ShapeDtypeStruct((m, d), BF16))
    return pl.pallas_call(
        functools.partial(_resid_kernel, with_mod=mod2 is not None),
        grid=(m // tm,),
        in_specs=in_specs, out_specs=out_specs, out_shape=out_shape,
        compiler_params=_params(("parallel",)),
        name="resid_norm",
    )(*args)


def _rope(x, c, sa, sb):
    half = ROPE_DIM // 2
    return x * c + pltpu.roll(x, half, 1) * sa + pltpu.roll(x, HEAD_DIM - half, 1) * sb


def _prep_kernel(proj_ref, lg_ref, c_ref, sa_ref, sb_ref,
                 qsb_ref, qn_ref, qr_ref,
                 sbk_ref, sbv_ref, sbk16_ref, sbv16_ref,
                 ck_ref, cv_ref,
                 sk_ref, sv_ref, sk16_ref, sv16_ref,
                 wk_ref, wv_ref, wk16_ref, wv16_ref,
                 gates_ref):
    c, sa, sb = c_ref[...], sa_ref[...], sb_ref[...]

    def head(off, h):
        return proj_ref[:, off + h * HEAD_DIM: off + (h + 1) * HEAD_DIM]

    for h in range(SB_HEADS):
        qsb_ref[h] = head(OFF_SB_Q, h).astype(BF16)
    for h in range(NSA_HEADS):
        q = head(OFF_NSA_Q, h)
        qn_ref[h] = q.astype(BF16)
        qr_ref[h] = _rope(q, c, sa, sb).astype(BF16)
    for h in range(SB_KV_HEADS):
        sl = slice(h * HEAD_DIM, (h + 1) * HEAD_DIM)
        k = head(OFF_SB_K, h)
        v = head(OFF_SB_V, h)
        sbk_ref[:, h, :] = k
        sbv_ref[:, h, :] = v
        sbk16_ref[:, sl] = k.astype(BF16)
        sbv16_ref[:, sl] = v.astype(BF16)
    for h in range(NSA_KV_HEADS):
        sl = slice(h * HEAD_DIM, (h + 1) * HEAD_DIM)
        ck_ref[:, h, :] = head(OFF_CMP_K, h)
        cv_ref[:, h, :] = head(OFF_CMP_V, h)
        ks = _rope(head(OFF_SLC_K, h), c, sa, sb)
        sk_ref[:, h, :] = ks
        sk16_ref[:, sl] = ks.astype(BF16)
        kw = _rope(head(OFF_WIN_K, h), c, sa, sb)
        wk_ref[:, h, :] = kw
        wk16_ref[:, sl] = kw.astype(BF16)
        vs = head(OFF_SLC_V, h)
        sv_ref[:, h, :] = vs
        sv16_ref[:, sl] = vs.astype(BF16)
        vw = head(OFF_WIN_V, h)
        wv_ref[:, h, :] = vw
        wv16_ref[:, sl] = vw.astype(BF16)
    gates_ref[...] = jax.nn.sigmoid(lg_ref[...])


def rope_tables(pos):
    half = ROPE_DIM // 2
    inv = ROPE_THETA ** (-jnp.arange(half, dtype=F32) / half)
    ang = pos.astype(F32)[:, None] * inv[None, :]
    cos, sin = jnp.cos(ang), jnp.sin(ang)
    n = pos.shape[0]
    rest = HEAD_DIM - ROPE_DIM
    c = jnp.concatenate([cos, cos, jnp.ones((n, rest), F32)], axis=1)
    sa = jnp.concatenate([jnp.zeros((n, half), F32), sin, jnp.zeros((n, rest), F32)], axis=1)
    sb = jnp.concatenate([-sin, jnp.zeros((n, half + rest), F32)], axis=1)
    return c, sa, sb


def prep_heads(proj, logits, pos, *, tm=256):
    m = proj.shape[0]
    tm = min(tm, m)
    c, sa, sb = rope_tables(pos)
    row = lambda w: pl.BlockSpec((tm, w), lambda i: (i, 0))
    hm = pl.BlockSpec((16, tm, HEAD_DIM), lambda i: (0, i, 0))
    nat = lambda nh: pl.BlockSpec((tm, nh, HEAD_DIM), lambda i: (i, 0, 0))
    f = lambda w, dt: jax.ShapeDtypeStruct((m, w), dt)
    fn = lambda nh: jax.ShapeDtypeStruct((m, nh, HEAD_DIM), F32)
    hshape = jax.ShapeDtypeStruct((16, m, HEAD_DIM), BF16)
    out_specs = [hm, hm, hm,
                 nat(4), nat(4), row(512), row(512),
                 nat(2), nat(2),
                 nat(2), nat(2), row(256), row(256),
                 nat(2), nat(2), row(256), row(256),
                 row(LANES)]
    out_shape = [hshape, hshape, hshape,
                 fn(4), fn(4), f(512, BF16), f(512, BF16),
                 fn(2), fn(2),
                 fn(2), fn(2), f(256, BF16), f(256, BF16),
                 fn(2), fn(2), f(256, BF16), f(256, BF16),
                 f(LANES, F32)]
    return pl.pallas_call(
        _prep_kernel,
        grid=(m // tm,),
        in_specs=[row(proj.shape[1]), row(LANES), row(LANES), row(LANES), row(LANES)],
        out_specs=out_specs, out_shape=out_shape,
        compiler_params=_params(("parallel",)),
        name="prep_heads",
    )(proj, logits, c, sa, sb)


def _cumsum_matrix():
    j = lax.broadcasted_iota(jnp.int32, (2 * LANES, 2 * LANES), 0) & (LANES - 1)
    s = lax.broadcasted_iota(jnp.int32, (2 * LANES, 2 * LANES), 1)
    return jnp.where((s >= LANES) | (j > s), 1.0, 0.0).astype(BF16)


def _split_hi_lo(x):
    hi = x.astype(BF16)
    lo = (x - hi.astype(F32)).astype(BF16)
    return jnp.concatenate([hi, lo], axis=1)


def _sb_block(z, causal, carry, um):
    sp = jnp.maximum(z, 0.0) + jnp.log1p(jnp.exp(-jnp.abs(z)))
    ls = jnp.where(causal, -sp, 0.0)
    cb = jnp.dot(_split_hi_lo(ls), um, preferred_element_type=F32)
    w = jnp.where(causal, jnp.exp(z + ls + cb[:, :LANES] + carry), 0.0)
    return w, carry + cb[:, LANES:]


def _softmax_rows(s, mask):
    s = jnp.where(mask, s, NEG)
    m = jnp.max(s, axis=-1, keepdims=True)
    p = jnp.where(mask, jnp.exp(s - m), 0.0)
    return p / jnp.maximum(jnp.sum(p, axis=-1, keepdims=True), 1e-30)


def _sb_prompt_kernel(q_ref, k_ref, v_ref, o_ref, acc_ref, carry_ref, *, tq):
    qi = pl.program_id(1)
    rows = SB_GROUP * tq
    q = q_ref[...].reshape(rows, HEAD_DIM)
    um = _cumsum_matrix()
    tk = 2 * LANES
    qpos = qi * tq + (lax.broadcasted_iota(jnp.int32, (rows, tk), 0) & (tq - 1))
    lane = lax.broadcasted_iota(jnp.int32, (rows, tk), 1)
    acc_ref[...] = jnp.zeros_like(acc_ref)
    carry_ref[...] = jnp.zeros_like(carry_ref)

    def body(state):
        kb, _ = state
        start = pl.multiple_of(kb * tk, tk)
        z = _nt(q, k_ref[pl.ds(start, tk), :]) * SCALE
        causal = (start + lane) < qpos
        sp = jnp.maximum(z, 0.0) + jnp.log1p(jnp.exp(-jnp.abs(z)))
        ls = jnp.where(causal, -sp, 0.0)
        cb = jnp.dot(_split_hi_lo(jnp.concatenate([ls[:, LANES:], ls[:, :LANES]], axis=0)), um,
                     preferred_element_type=F32)
        carry = carry_ref[...]
        newer_total = cb[:rows, LANES:]
        log_after = jnp.concatenate([cb[rows:, :LANES] + (carry + newer_total), cb[:rows, :LANES] + carry], axis=1)
        w = jnp.where(causal, jnp.exp(z + ls + log_after), 0.0)
        acc_ref[...] += jnp.dot(w.astype(BF16), v_ref[pl.ds(start, tk), :], preferred_element_type=F32)
        carry = carry + newer_total + cb[rows:, LANES:]
        carry_ref[...] = carry
        return kb - 1, jnp.max(carry)

    lax.while_loop(lambda st: (st[0] >= 0) & (st[1] > SB_DEAD), body,
                   ((qi * tq + tq - 1) // tk, jnp.float32(0.0)))
    o_ref[...] = acc_ref[...].reshape(1, SB_GROUP, tq, HEAD_DIM)


def sb_prompt(q, k16, v16, *, tq=256):
    t = q.shape[1]
    qv = q.reshape(SB_KV_HEADS, SB_GROUP, t, HEAD_DIM)
    out = pl.pallas_call(
        functools.partial(_sb_prompt_kernel, tq=tq),
        grid=(SB_KV_HEADS, t // tq),
        in_specs=[pl.BlockSpec((1, SB_GROUP, tq, HEAD_DIM), lambda h, i: (h, 0, i, 0)),
                  pl.BlockSpec((t, HEAD_DIM), lambda h, i: (0, h)),
                  pl.BlockSpec((t, HEAD_DIM), lambda h, i: (0, h))],
        out_specs=pl.BlockSpec((1, SB_GROUP, tq, HEAD_DIM), lambda h, i: (h, 0, i, 0)),
        out_shape=jax.ShapeDtypeStruct((SB_KV_HEADS, SB_GROUP, t, HEAD_DIM), F32),
        scratch_shapes=[pltpu.VMEM((SB_GROUP * tq, HEAD_DIM), F32),
                        pltpu.VMEM((SB_GROUP * tq, LANES), F32)],
        compiler_params=_params(("parallel", "arbitrary")),
        name="sb_prompt",
    )(qv, k16, v16)
    return out.reshape(SB_HEADS, t, HEAD_DIM)


def _compress(get_chunk_rows, w1_ref, b_ref, w2_ref, n_chunks):
    rows = NSA_KV_HEADS * n_chunks
    ab = jnp.zeros((rows, 2 * CMP_HIDDEN), F32)
    for jj in range(CMP_STRIDE // 2):
        x = jnp.concatenate(
            [jnp.concatenate([get_chunk_rows(h, 2 * jj), get_chunk_rows(h, 2 * jj + 1)], axis=1)
             for h in range(NSA_KV_HEADS)], axis=0)
        ab = ab + jnp.dot(x.astype(BF16), w1_ref[jj], preferred_element_type=F32)
    first = ab[:, :CMP_HIDDEN]
    second = ab[:, CMP_HIDDEN:]
    nxt = jnp.concatenate([second[1:], second[:1]], axis=0)
    hidden = first + nxt + b_ref[...]
    return jnp.dot(jax.nn.gelu(hidden).astype(BF16), w2_ref[...], preferred_element_type=F32)


def _cmp_weights(w1, w2):
    half = CMP_STRIDE * HEAD_DIM
    a = w1[:half].reshape(CMP_STRIDE // 2, 2 * HEAD_DIM, CMP_HIDDEN)
    b = w1[half:].reshape(CMP_STRIDE // 2, 2 * HEAD_DIM, CMP_HIDDEN)
    return jnp.concatenate([a, b], axis=-1).astype(BF16), w2.astype(BF16)


def _pe_bias(pe, w1):
    row = jnp.zeros((8, pe.size), F32).at[0].set(pe.reshape(-1))
    return matmul(row, w1, tm=8, tn=CMP_HIDDEN, name="cmp_pe_bias")[:1]


def _compress_prompt_kernel(x_ref, w1_ref, b_ref, w2_ref, o_ref, *, n_chunks):
    get = lambda h, j: x_ref[pl.ds(j, n_chunks, stride=CMP_STRIDE), h, :]
    out = _compress(get, w1_ref.at[0], b_ref.at[0], w2_ref.at[0], n_chunks)
    o_ref[0] = out.reshape(NSA_KV_HEADS, n_chunks, HEAD_DIM).astype(o_ref.dtype)


def compress_prompt(ck, cv, w1s, bs, w2s):
    t = ck.shape[0]
    n_chunks = t // CMP_STRIDE
    x = jnp.stack([ck, cv])
    return pl.pallas_call(
        functools.partial(_compress_prompt_kernel, n_chunks=n_chunks),
        grid=(2,),
        in_specs=[pl.BlockSpec((None, t, NSA_KV_HEADS, HEAD_DIM), lambda a: (a, 0, 0, 0)),
                  pl.BlockSpec((1, CMP_STRIDE // 2, 2 * HEAD_DIM, 2 * CMP_HIDDEN), lambda a: (a, 0, 0, 0)),
                  pl.BlockSpec((1, 1, CMP_HIDDEN), lambda a: (a, 0, 0)),
                  pl.BlockSpec((1, CMP_HIDDEN, HEAD_DIM), lambda a: (a, 0, 0))],
        out_specs=pl.BlockSpec((1, NSA_KV_HEADS, n_chunks, HEAD_DIM), lambda a: (a, 0, 0, 0)),
        out_shape=jax.ShapeDtypeStruct((2, NSA_KV_HEADS, n_chunks, HEAD_DIM), BF16),
        compiler_params=_params(("parallel",)),
        name="compress_prompt",
    )(x, w1s, bs, w2s)


def _overlap_t(n_cmp_pad):
    n = lax.broadcasted_iota(jnp.int32, (LANES, n_cmp_pad), 0)
    c0 = lax.broadcasted_iota(jnp.int32, (LANES, n_cmp_pad), 1) * CMP_STRIDE
    s0 = n * SLC_BLOCK
    return jnp.where((c0 < s0 + SLC_BLOCK) & (c0 + CMP_BLOCK > s0), 1.0, 0.0).astype(BF16)


def _select_bias_t(imp_t, qpos_row, score_ref, n_live):
    shape = imp_t.shape
    blk = lax.broadcasted_iota(jnp.int32, shape, 0)
    cur = qpos_row >> SLC_SHIFT
    valid = blk <= cur
    forced = valid & ((blk == 0) | (blk > cur - N_LOCAL_BLOCKS))
    score = jnp.where(valid, imp_t + jnp.where(forced, FORCE_BONUS, 0.0), -1.0)
    score_ref[...] = score

    def body(m, rank):
        sm = score_ref[pl.ds(m, 1), :]
        return rank + jnp.where(blk > m, jnp.where(sm >= score, 1.0, 0.0), jnp.where(sm > score, 1.0, 0.0))

    rank = lax.fori_loop(0, n_live, body, jnp.zeros(shape, F32))
    return jnp.where(valid & (rank < SLC_TOPK), 0.0, MASK_BIAS)


def _cmp_prompt_kernel(q_ref, kc_ref, vc_ref, o_ref, bias_ref, score_ref, *, tq, n_cmp):
    qi = pl.program_id(1)
    rows = NSA_GROUP * tq
    q = q_ref[...].reshape(rows, HEAD_DIM)
    kc = kc_ref[0, 0]
    vc = vc_ref[0, 0]
    n_pad = kc.shape[0]
    s = _nt(q, kc) * SCALE
    qpos = qi * tq + (lax.broadcasted_iota(jnp.int32, (rows, n_pad), 0) & (tq - 1))
    c = lax.broadcasted_iota(jnp.int32, (rows, n_pad), 1)
    mask = (c * CMP_STRIDE + (CMP_BLOCK - 1) <= qpos) & (c < n_cmp)
    p = _softmax_rows(s, mask)
    o = jnp.dot(p.astype(BF16), vc, preferred_element_type=F32)
    o_ref[...] = o.reshape(1, NSA_GROUP, tq, HEAD_DIM)
    psum = p[:tq]
    for g in range(1, NSA_GROUP):
        psum = psum + p[g * tq:(g + 1) * tq]
    hi = psum.astype(BF16)
    lo = (psum - hi.astype(F32)).astype(BF16)
    ov = _overlap_t(n_pad)
    imp_t = _nt(ov, hi) + _nt(ov, lo)
    qpos_row = qi * tq + lax.broadcasted_iota(jnp.int32, (1, tq), 1)
    n_live = jnp.minimum((qi * tq + tq - 1) // SLC_BLOCK + 1, LANES)
    bias_ref[0] = _select_bias_t(imp_t, qpos_row, score_ref, n_live)


def cmp_prompt(q, kvc, *, tq=128):
    t = q.shape[1]
    n_chunks = kvc.shape[2]
    qv = q.reshape(NSA_KV_HEADS, NSA_GROUP, t, HEAD_DIM)
    o, bias_t = pl.pallas_call(
        functools.partial(_cmp_prompt_kernel, tq=tq, n_cmp=n_chunks - 1),
        grid=(NSA_KV_HEADS, t // tq),
        in_specs=[pl.BlockSpec((1, NSA_GROUP, tq, HEAD_DIM), lambda h, i: (h, 0, i, 0)),
                  pl.BlockSpec((1, 1, n_chunks, HEAD_DIM), lambda h, i: (0, h, 0, 0)),
                  pl.BlockSpec((1, 1, n_chunks, HEAD_DIM), lambda h, i: (1, h, 0, 0))],
        out_specs=[pl.BlockSpec((1, NSA_GROUP, tq, HEAD_DIM), lambda h, i: (h, 0, i, 0)),
                   pl.BlockSpec((1, LANES, tq), lambda h, i: (h, 0, i))],
        out_shape=[jax.ShapeDtypeStruct((NSA_KV_HEADS, NSA_GROUP, t, HEAD_DIM), F32),
                   jax.ShapeDtypeStruct((NSA_KV_HEADS, LANES, t), F32)],
        scratch_shapes=[pltpu.VMEM((LANES, tq), F32)],
        compiler_params=_params(("parallel", "parallel")),
        name="cmp_prompt",
    )(qv, kvc, kvc)
    return o.reshape(NSA_HEADS, t, HEAD_DIM), bias_t


def _slc_prompt_kernel(q_ref, bias_ref, k_ref, v_ref, o_ref, m_ref, l_ref, acc_ref, *, tq, tk):
    qi = pl.program_id(1)
    rows = NSA_GROUP * tq
    reps = tk // LANES
    q = q_ref[...].reshape(rows, HEAD_DIM)
    bias = bias_ref[0]
    qa = jnp.concatenate([q, jnp.concatenate([bias] * NSA_GROUP, axis=0)], axis=1)
    blk_row = lax.broadcasted_iota(jnp.int32, (tk, LANES), 0) >> SLC_SHIFT
    blk_col = lax.broadcasted_iota(jnp.int32, (tk, LANES), 1)
    diag = (qi * tq) // tk

    def scores(kb):
        start = pl.multiple_of(kb * tk, tk)
        onehot = jnp.where(blk_row + kb * (tk // SLC_BLOCK) == blk_col, 1.0, 0.0).astype(BF16)
        ka = jnp.concatenate([k_ref[pl.ds(start, tk), :], onehot], axis=1)
        return _nt(qa, ka), v_ref[pl.ds(start, tk), :]

    s, v = scores(diag)
    qpos = qi * tq + (lax.broadcasted_iota(jnp.int32, (rows, tk), 0) & (tq - 1))
    kpos = diag * tk + lax.broadcasted_iota(jnp.int32, (rows, tk), 1)
    s = jnp.where(kpos <= qpos, s, NEG)
    m0 = jnp.broadcast_to(jnp.max(s, axis=1, keepdims=True), (rows, LANES))
    p = jnp.exp((s - jnp.concatenate([m0] * reps, axis=1)) * SCALE)
    m_ref[...] = m0
    l_ref[...] = jnp.broadcast_to(jnp.sum(p, axis=1, keepdims=True), (rows, LANES))
    acc_ref[...] = jnp.dot(p.astype(BF16), v, preferred_element_type=F32)

    def body(i, carry):
        s, v = scores(diag - 1 - i)
        m_old = m_ref[...]
        m_new = jnp.maximum(m_old, jnp.max(s, axis=1, keepdims=True))
        alpha = jnp.exp((m_old - m_new) * SCALE)
        p = jnp.exp((s - jnp.concatenate([m_new] * reps, axis=1)) * SCALE)
        l_ref[...] = alpha * l_ref[...] + jnp.sum(p, axis=1, keepdims=True)
        acc_ref[...] = alpha * acc_ref[...] + jnp.dot(p.astype(BF16), v, preferred_element_type=F32)
        m_ref[...] = m_new
        return carry

    lax.fori_loop(0, diag, body, 0)
    o = acc_ref[...] / l_ref[...]
    o_ref[...] = o.reshape(1, NSA_GROUP, tq, HEAD_DIM)


def slc_prompt(q, bias, k16, v16, *, tq=128, tk=512):
    t = q.shape[1]
    qv = q.reshape(NSA_KV_HEADS, NSA_GROUP, t, HEAD_DIM)
    rows = NSA_GROUP * tq
    out = pl.pallas_call(
        functools.partial(_slc_prompt_kernel, tq=tq, tk=tk),
        grid=(NSA_KV_HEADS, t // tq),
        in_specs=[pl.BlockSpec((1, NSA_GROUP, tq, HEAD_DIM), lambda h, i: (h, 0, i, 0)),
                  pl.BlockSpec((1, tq, LANES), lambda h, i: (h, i, 0)),
                  pl.BlockSpec((t, HEAD_DIM), lambda h, i: (0, h)),
                  pl.BlockSpec((t, HEAD_DIM), lambda h, i: (0, h))],
        out_specs=pl.BlockSpec((1, NSA_GROUP, tq, HEAD_DIM), lambda h, i: (h, 0, i, 0)),
        out_shape=jax.ShapeDtypeStruct((NSA_KV_HEADS, NSA_GROUP, t, HEAD_DIM), F32),
        scratch_shapes=[pltpu.VMEM((rows, LANES), F32), pltpu.VMEM((rows, LANES), F32),
                        pltpu.VMEM((rows, HEAD_DIM), F32)],
        compiler_params=_params(("parallel", "arbitrary")),
        name="slc_prompt",
    )(qv, bias, k16, v16)
    return out.reshape(NSA_HEADS, t, HEAD_DIM)


def _win_prompt_kernel(q_ref, k_ref, v_ref, o_ref, *, tq):
    qi = pl.program_id(1)
    rows = NSA_GROUP * tq
    span = WINDOW + tq
    q = q_ref[...].reshape(rows, HEAD_DIM)
    start = pl.multiple_of(jnp.maximum(qi * tq - WINDOW, 0), tq)
    s = _nt(q, k_ref[pl.ds(start, span), :])
    qpos = qi * tq + (lax.broadcasted_iota(jnp.int32, (rows, span), 0) & (tq - 1))
    dist = qpos - (start + lax.broadcasted_iota(jnp.int32, (rows, span), 1))
    s = jnp.where((dist >= 0) & (dist < WINDOW), s, NEG)
    p = jnp.exp((s - jnp.max(s, axis=1, keepdims=True)) * SCALE)
    o = jnp.dot(p.astype(BF16), v_ref[pl.ds(start, span), :], preferred_element_type=F32)
    o = o / jnp.sum(p, axis=1, keepdims=True)
    o_ref[...] = o.reshape(1, NSA_GROUP, tq, HEAD_DIM)


def win_prompt(q, k16, v16, *, tq=128):
    t = q.shape[1]
    qv = q.reshape(NSA_KV_HEADS, NSA_GROUP, t, HEAD_DIM)
    rows = NSA_GROUP * tq
    out = pl.pallas_call(
        functools.partial(_win_prompt_kernel, tq=tq),
        grid=(NSA_KV_HEADS, t // tq),
        in_specs=[pl.BlockSpec((1, NSA_GROUP, tq, HEAD_DIM), lambda h, i: (h, 0, i, 0)),
                  pl.BlockSpec((t, HEAD_DIM), lambda h, i: (0, h)),
                  pl.BlockSpec((t, HEAD_DIM), lambda h, i: (0, h))],
        out_specs=pl.BlockSpec((1, NSA_GROUP, tq, HEAD_DIM), lambda h, i: (h, 0, i, 0)),
        out_shape=jax.ShapeDtypeStruct((NSA_KV_HEADS, NSA_GROUP, t, HEAD_DIM), F32),
        compiler_params=_params(("parallel", "arbitrary")),
        name="win_prompt",
    )(qv, k16, v16)
    return out.reshape(NSA_HEADS, t, HEAD_DIM)


def _combine_kernel(osb_ref, oc_ref, os_ref, ow_ref, gates_ref, gsb_ref, gnsa_ref, out_ref):
    gsb = gsb_ref[...]
    gnsa = gnsa_ref[...]
    gates = gates_ref[...]
    for h in range(SB_HEADS):
        out_ref[:, h * HEAD_DIM:(h + 1) * HEAD_DIM] = (_rms(osb_ref[h]) * gsb).astype(out_ref.dtype)
    for h in range(NSA_HEADS):
        o = (gates[:, h:h + 1] * oc_ref[h]
             + gates[:, NSA_HEADS + h:NSA_HEADS + h + 1] * os_ref[h]
             + gates[:, 2 * NSA_HEADS + h:2 * NSA_HEADS + h + 1] * ow_ref[h])
        col = (SB_HEADS + h) * HEAD_DIM
        out_ref[:, col:col + HEAD_DIM] = (_rms(o) * gnsa).astype(out_ref.dtype)


def combine_heads(o_sb, o_cmp, o_slc, o_win, gates, g_sb, g_nsa, *, tm=256):
    m = o_sb.shape[1]
    tm = min(tm, m)
    hm = pl.BlockSpec((16, tm, HEAD_DIM), lambda i: (0, i, 0))
    vec = pl.BlockSpec((1, HEAD_DIM), lambda i: (0, 0))
    width = (SB_HEADS + NSA_HEADS) * HEAD_DIM
    return pl.pallas_call(
        _combine_kernel,
        grid=(m // tm,),
        in_specs=[hm, hm, hm, hm, pl.BlockSpec((tm, LANES), lambda i: (i, 0)), vec, vec],
        out_specs=pl.BlockSpec((tm, width), lambda i: (i, 0)),
        out_shape=jax.ShapeDtypeStruct((m, width), BF16),
        compiler_params=_params(("parallel",)),
        name="combine_heads",
    )(o_sb, o_cmp, o_slc, o_win, gates, g_sb.reshape(1, -1), g_nsa.reshape(1, -1))


def _pool_rows(pool):
    d, n_pool, page, nh, hd = pool.shape
    return pool.reshape(d * n_pool, page * nh, hd)


def _page_specs(n_pages, n_heads, page_offset):
    return [pl.BlockSpec((None, PAGE * n_heads, HEAD_DIM),
                         functools.partial(lambda p, s, pt: (page_offset + pt[s, p], 0, 0), p))
            for p in range(n_pages)]


def _head_rows(ref, h, n_heads, n_rows=PAGE, first=0, every=1):
    return ref[pl.ds(first * n_heads + h, n_rows, stride=every * n_heads), :]


def _gather_head(page_refs, new_ref, h, n_heads):
    parts = [_head_rows(r, h, n_heads).astype(BF16) for r in page_refs]
    parts.append(new_ref[0, :, h * HEAD_DIM:(h + 1) * HEAD_DIM])
    return jnp.concatenate(parts, axis=0)


def _sb_sample_body(q_ref, kn_ref, vn_ref, k_pages, v_pages, o_ref, *, n_pages, n_new):
    past = n_pages * PAGE
    nb = n_pages + 1
    rows = n_new * SB_GROUP
    grp = SB_KV_HEADS * rows
    um = _cumsum_matrix()
    zs = [_nt(q_ref[0, h], _gather_head(k_pages, kn_ref, h, SB_KV_HEADS)) for h in range(SB_KV_HEADS)]
    z = jnp.concatenate([zh[:, b * PAGE:(b + 1) * PAGE] for b in range(nb) for zh in zs], axis=0) * SCALE
    i = lax.broadcasted_iota(jnp.int32, (nb * grp, LANES), 0)
    assert grp & (grp - 1) == 0, "new tokens per sequence must be a power of two"
    kpos = (i >> (grp.bit_length() - 1)) * PAGE + lax.broadcasted_iota(jnp.int32, (nb * grp, LANES), 1)
    qpos = past + ((i & (rows - 1)) >> SB_GROUP_SHIFT)
    causal = kpos < qpos
    sp = jnp.maximum(z, 0.0) + jnp.log1p(jnp.exp(-jnp.abs(z)))
    ls = jnp.where(causal, -sp, 0.0)
    cb = jnp.dot(_split_hi_lo(ls), um, preferred_element_type=F32)
    carries = [None] * nb
    carry = jnp.zeros((grp, LANES), F32)
    for b in range(nb - 1, -1, -1):
        carries[b] = carry
        carry = carry + cb[b * grp:(b + 1) * grp, LANES:]
    w = jnp.where(causal, jnp.exp(z + ls + cb[:, :LANES] + jnp.concatenate(carries, axis=0)), 0.0)
    for h in range(SB_KV_HEADS):
        wh = jnp.concatenate([w[b * grp + h * rows:b * grp + (h + 1) * rows] for b in range(nb)], axis=1)
        vh = _gather_head(v_pages, vn_ref, h, SB_KV_HEADS)
        o_ref[0, h] = jnp.dot(wh.astype(BF16), vh, preferred_element_type=F32)


def _cmp_sample_body(q_ref, kn_ref, vn_ref, w1_ref, b_ref, w2_ref, k_pages, v_pages, o_ref, *, n_pages, n_new):
    per_page = PAGE // CMP_STRIDE
    past = n_pages * PAGE
    n_chunks = n_pages * per_page + 8
    n_cmp = n_chunks - 8
    rows = n_new * NSA_GROUP

    def summaries(pages, new_ref, which):
        tok_major = [[jnp.swapaxes(_head_rows(r, h, NSA_KV_HEADS).reshape(per_page, CMP_STRIDE, HEAD_DIM), 0, 1)
                      for r in pages] for h in range(NSA_KV_HEADS)]

        def get(h, j):
            parts = [y[j] for y in tok_major[h]]
            parts.append(new_ref[0, j][:, h * HEAD_DIM:(h + 1) * HEAD_DIM])
            return jnp.concatenate(parts, axis=0)

        return _compress(get, w1_ref.at[which], b_ref.at[which], w2_ref.at[which], n_chunks)

    kc_all = summaries(k_pages, kn_ref, 0)
    vc_all = summaries(v_pages, vn_ref, 1)

    c = lax.broadcasted_iota(jnp.int32, (rows, n_chunks), 1)
    t_row = lax.broadcasted_iota(jnp.int32, (rows, n_chunks), 0) >> NSA_GROUP_SHIFT
    mask = (c * CMP_STRIDE + (CMP_BLOCK - 1) <= past + t_row) & (c < n_cmp)
    cc = lax.broadcasted_iota(jnp.int32, (n_chunks, LANES), 0) * CMP_STRIDE
    s0 = lax.broadcasted_iota(jnp.int32, (n_chunks, LANES), 1) * SLC_BLOCK
    ov = jnp.where((cc < s0 + SLC_BLOCK) & (cc + CMP_BLOCK > s0), 1.0, 0.0).astype(BF16)
    blk = lax.broadcasted_iota(jnp.int32, (n_new, LANES), 1)
    cur = (past + lax.broadcasted_iota(jnp.int32, (n_new, LANES), 0)) >> SLC_SHIFT
    valid = blk <= cur
    forced = valid & ((blk == 0) | (blk > cur - N_LOCAL_BLOCKS))
    n_slc = past // SLC_BLOCK + -(-n_new // SLC_BLOCK)
    biases = []
    for h in range(NSA_KV_HEADS):
        kc = kc_all[h * n_chunks:(h + 1) * n_chunks].astype(BF16)
        vc = vc_all[h * n_chunks:(h + 1) * n_chunks].astype(BF16)
        p = _softmax_rows(_nt(q_ref[0, h], kc) * SCALE, mask)
        o_ref[0, h] = jnp.dot(p.astype(BF16), vc, preferred_element_type=F32)
        psum = jnp.sum(p.reshape(n_new, NSA_GROUP, n_chunks), axis=1)
        hi = psum.astype(BF16)
        lo = (psum - hi.astype(F32)).astype(BF16)
        imp = jnp.dot(hi, ov, preferred_element_type=F32) + jnp.dot(lo, ov, preferred_element_type=F32)
        score = jnp.where(valid, imp + jnp.where(forced, FORCE_BONUS, 0.0), -1.0)
        rank = jnp.zeros((n_new, LANES), F32)
        for m in range(n_slc):
            sm = score[:, m:m + 1]
            rank = rank + jnp.where(blk > m, jnp.where(sm >= score, 1.0, 0.0), jnp.where(sm > score, 1.0, 0.0))
        biases.append(jnp.where(valid & (rank < SLC_TOPK), 0.0, MASK_BIAS))
    return biases


def _slc_sample_body(q_ref, biases, kn_ref, vn_ref, k_pages, v_pages, o_ref, *, n_pages, n_new):
    past = n_pages * PAGE
    nk = (n_pages + 1) * PAGE
    rows = n_new * NSA_GROUP
    kpos = lax.broadcasted_iota(jnp.int32, (rows, nk), 1)
    qpos = past + (lax.broadcasted_iota(jnp.int32, (rows, nk), 0) >> NSA_GROUP_SHIFT)
    blk_of_key = lax.broadcasted_iota(jnp.int32, (LANES, nk), 1) >> SLC_SHIFT
    expand = jnp.where(blk_of_key == lax.broadcasted_iota(jnp.int32, (LANES, nk), 0), 1.0, 0.0).astype(BF16)
    for h in range(NSA_KV_HEADS):
        kh = _gather_head(k_pages, kn_ref, h, NSA_KV_HEADS)
        vh = _gather_head(v_pages, vn_ref, h, NSA_KV_HEADS)
        bias = jnp.concatenate([jnp.broadcast_to(biases[h][t:t + 1], (NSA_GROUP, LANES)) for t in range(n_new)],
                               axis=0).astype(BF16)
        s = (_nt(q_ref[0, h], kh) + jnp.dot(bias, expand, preferred_element_type=F32)) * SCALE
        mask = (kpos <= qpos) & (s > 0.5 * MASK_BIAS * SCALE)
        p = _softmax_rows(s, mask)
        o_ref[0, h] = jnp.dot(p.astype(BF16), vh, preferred_element_type=F32)


def _win_sample_body(q_ref, kn_ref, vn_ref, kb_ref, vb_ref, o_ref, ko_ref, vo_ref, *, n_new, past):
    nh = NSA_KV_HEADS
    win = kb_ref.shape[0] // nh
    rows = n_new * NSA_GROUP
    nk = win + 8
    kpos = past - win + lax.broadcasted_iota(jnp.int32, (rows, nk), 1)
    qpos = past + (lax.broadcasted_iota(jnp.int32, (rows, nk), 0) >> NSA_GROUP_SHIFT)
    dist = qpos - kpos
    mask = (dist >= 0) & (dist < WINDOW) & (kpos >= 0)
    for h in range(nh):
        sl = slice(h * HEAD_DIM, (h + 1) * HEAD_DIM)
        kall = jnp.concatenate([_head_rows(kb_ref, h, nh, win), kn_ref[0][:, sl]], axis=0)
        vall = jnp.concatenate([_head_rows(vb_ref, h, nh, win), vn_ref[0][:, sl]], axis=0)
        ko_ref[pl.ds(h, win, stride=nh), :] = kall[n_new:n_new + win]
        vo_ref[pl.ds(h, win, stride=nh), :] = vall[n_new:n_new + win]
        p = _softmax_rows(_nt(q_ref[0, h], kall.astype(BF16)) * SCALE, mask)
        o_ref[0, h] = jnp.dot(p.astype(BF16), vall.astype(BF16), preferred_element_type=F32)


def _sample_mixers_kernel(pt_ref, q_sb_ref, q_nsa_ref, q_rot_ref,
                          sbk_new_ref, sbv_new_ref, ck_new_ref, cv_new_ref, sk_new_ref, sv_new_ref,
                          wk_new_ref, wv_new_ref, w1_ref, b_ref, w2_ref, wk_buf_ref, wv_buf_ref,
                          *refs, n_pages, n_new):
    pools = [refs[i * n_pages:(i + 1) * n_pages] for i in range(6)]
    o_sb_ref, o_cmp_ref, o_slc_ref, o_win_ref, wk_out_ref, wv_out_ref = refs[6 * n_pages:]
    sizes = dict(n_pages=n_pages, n_new=n_new)
    _sb_sample_body(q_sb_ref, sbk_new_ref, sbv_new_ref, pools[0], pools[1], o_sb_ref, **sizes)
    biases = _cmp_sample_body(q_nsa_ref, ck_new_ref, cv_new_ref, w1_ref, b_ref, w2_ref, pools[2], pools[3],
                              o_cmp_ref, **sizes)
    _slc_sample_body(q_rot_ref, biases, sk_new_ref, sv_new_ref, pools[4], pools[5], o_slc_ref, **sizes)
    _win_sample_body(q_rot_ref, wk_new_ref, wv_new_ref, wk_buf_ref, wv_buf_ref, o_win_ref, wk_out_ref, wv_out_ref,
                     n_new=n_new, past=n_pages * PAGE)


def sample_mixers(page_table, q_sb, q_nsa, q_rot, new_rows, cmp_w, caches, layer):
    b, n_pages = page_table.shape
    sb_rows, nsa_rows = q_sb.shape[2], q_nsa.shape[2]
    n_new = nsa_rows // NSA_GROUP
    sbk_new, sbv_new, ck_new, cv_new, sk_new, sv_new, wk_new, wv_new = new_rows
    sb_k_pool, sb_v_pool, cmp_k_pool, cmp_v_pool, slc_k_pool, slc_v_pool, wk_buf, wv_buf = caches
    _, _, win, nh, _ = wk_buf.shape
    per_seq = lambda a: pl.BlockSpec((1,) + a.shape[1:], lambda s, pt: (s,) + (0,) * (a.ndim - 1))
    full = lambda a: pl.BlockSpec(a.shape, lambda s, pt: (0,) * a.ndim)
    buf_in = pl.BlockSpec((None, win * nh, HEAD_DIM), lambda s, pt: (layer * b + s, 0, 0))
    buf_out = pl.BlockSpec((None, win * nh, HEAD_DIM), lambda s, pt: (s, 0, 0))
    buf_shape = jax.ShapeDtypeStruct((b, win * nh, HEAD_DIM), F32)
    sb_out = jax.ShapeDtypeStruct((b, SB_KV_HEADS, sb_rows, HEAD_DIM), F32)
    nsa_out = jax.ShapeDtypeStruct((b, NSA_KV_HEADS, nsa_rows, HEAD_DIM), F32)
    pools = (sb_k_pool, sb_v_pool, cmp_k_pool, cmp_v_pool, slc_k_pool, slc_v_pool)
    page_specs, page_args = [], []
    for pool in pools:
        page_specs += _page_specs(n_pages, pool.shape[3], layer * pool.shape[1])
        page_args += [_pool_rows(pool)] * n_pages
    singles = (q_sb, q_nsa, q_rot, sbk_new, sbv_new, ck_new, cv_new, sk_new, sv_new, wk_new, wv_new)
    grid_spec = pltpu.PrefetchScalarGridSpec(
        num_scalar_prefetch=1, grid=(b,),
        in_specs=[per_seq(a) for a in singles] + [full(a) for a in cmp_w] + [buf_in, buf_in] + page_specs,
        out_specs=[per_seq(sb_out), per_seq(nsa_out), per_seq(nsa_out), per_seq(nsa_out), buf_out, buf_out])
    o_sb, o_cmp, o_slc, o_win, k_out, v_out = pl.pallas_call(
        functools.partial(_sample_mixers_kernel, n_pages=n_pages, n_new=n_new),
        grid_spec=grid_spec,
        out_shape=[sb_out, nsa_out, nsa_out, nsa_out, buf_shape, buf_shape],
        compiler_params=_params(("parallel",)),
        name="sample_mixers",
    )(page_table, *singles, *cmp_w, _pool_rows(wk_buf), _pool_rows(wv_buf), *page_args)
    return o_sb, o_cmp, o_slc, o_win, k_out.reshape(b, win, nh, HEAD_DIM), v_out.reshape(b, win, nh, HEAD_DIM)


def _project(h, w_in):
    w_t = jnp.swapaxes(w_in, 0, 1)
    proj = matmul(h, w_t, tm=1024, tn=512, n_cols=OFF_GATE, w_rows_are_outputs=True, name="in_proj")
    w_gate_logits = jnp.pad(w_t[OFF_GATE:], ((0, LANES - N_GATE), (0, 0)))
    logits = matmul(h, w_gate_logits, tm=1024, tn=LANES, w_rows_are_outputs=True, name="gate_logits")
    return proj, logits


def _finish_layer(x, mixed, mod, w):
    shift_m, scale_m, gate_m, shift_f, scale_f, gate_f = mod
    o = matmul(mixed, w['w_o'], tm=1024, tn=512, name="out_proj")
    x1, h2 = resid_norm(x, o, w['g_post_mix'], gate_m, (w['g_pre_ffn'], scale_f, shift_f))
    act = gate_up(h2, w['w_gate'], w['w_up'], tm=2048, tn=256)
    f = matmul(act, w['w_down'], tm=1024, tn=256, a_single_buffer=True, name="ffn_down")
    return resid_norm(x1, f, w['g_post_ffn'], gate_f)[0]


def _prompt_layer(x, mod, w, cmp_w):
    t = x.shape[0]
    shift_m, scale_m = mod[0], mod[1]
    h = norm_mod(x, w['g_pre_mix'], scale_m, shift_m)
    proj, logits = _project(h, w['w_in'])
    (q_sb, q_nsa, q_rot, sbk, sbv, sbk16, sbv16, ck, cv, sk, sv, sk16, sv16,
     wk, wv, wk16, wv16, gates) = prep_heads(proj, logits, jnp.arange(t))
    o_sb = sb_prompt(q_sb, sbk16, sbv16)
    kvc = compress_prompt(ck, cv, *cmp_w)
    o_cmp, bias_t = cmp_prompt(q_nsa, kvc)
    bias = jnp.swapaxes(bias_t, 1, 2).astype(BF16)
    o_slc = slc_prompt(q_rot, bias, sk16, sv16)
    o_win = win_prompt(q_rot, wk16, wv16)
    mixed = combine_heads(o_sb, o_cmp, o_slc, o_win, gates, w['g_out_sb'], w['g_out_nsa'])
    y = _finish_layer(x, mixed, mod, w)
    win = min(WINDOW, t)
    states = tuple(a[None] for a in (sbk, sbv, ck, cv, sk, sv, wk[t - win:], wv[t - win:]))
    return y, states


def _seq_rows(a, b, n_new, group):
    kvh = a.shape[0] // group
    a = a.reshape(kvh, group, n_new, b, HEAD_DIM)
    return a.transpose(3, 0, 2, 1, 4).reshape(b, kvh, n_new * group, HEAD_DIM)


def _head_major(o, b, n_new, group):
    kvh = o.shape[1]
    o = o.reshape(b, kvh, n_new, group, HEAD_DIM)
    return o.transpose(1, 3, 2, 0, 4).reshape(kvh * group, n_new * b, HEAD_DIM)


def _per_seq(a, b, n_new):
    return jnp.swapaxes(a.reshape(n_new, b, *a.shape[1:]), 0, 1)


def _pad_rows(a, b, n_new, n_rows):
    a = _per_seq(a, b, n_new).reshape(b, n_new, -1)
    return jnp.pad(a, ((0, 0), (0, n_rows - n_new), (0, 0)))


def _sample_layer(x, mod, w, cmp_w, caches, page_table, layer):
    b, n_pages = page_table.shape
    m = x.shape[0]
    n_new = m // b
    past = n_pages * PAGE
    sb_k_pool, sb_v_pool, cmp_k_pool, cmp_v_pool, slc_k_pool, slc_v_pool, win_k_buf, win_v_buf = caches
    shift_m, scale_m = mod[0], mod[1]
    h = norm_mod(x, w['g_pre_mix'], scale_m, shift_m)
    proj, logits = _project(h, w['w_in'])
    pos = jnp.repeat(past + jnp.arange(n_new), b)
    (q_sb, q_nsa, q_rot, sbk, sbv, sbk16, sbv16, ck, cv, sk, sv, sk16, sv16,
     wk, wv, wk16, wv16, gates) = prep_heads(proj, logits, pos)

    chunk_new = lambda a: jnp.pad(_pad_rows(a, b, n_new, CMP_STRIDE)[:, :, None, :],
                                  ((0, 0), (0, 0), (0, 7), (0, 0)))
    new_rows = (_pad_rows(sbk16, b, n_new, PAGE), _pad_rows(sbv16, b, n_new, PAGE),
                chunk_new(ck), chunk_new(cv),
                _pad_rows(sk16, b, n_new, PAGE), _pad_rows(sv16, b, n_new, PAGE),
                _pad_rows(wk, b, n_new, 8), _pad_rows(wv, b, n_new, 8))
    o_sb, o_cmp, o_slc, o_win, win_k_new, win_v_new = sample_mixers(
        page_table, _seq_rows(q_sb, b, n_new, SB_GROUP), _seq_rows(q_nsa, b, n_new, NSA_GROUP),
        _seq_rows(q_rot, b, n_new, NSA_GROUP), new_rows, cmp_w, caches, layer)
    mixed = combine_heads(_head_major(o_sb, b, n_new, SB_GROUP), _head_major(o_cmp, b, n_new, NSA_GROUP),
                          _head_major(o_slc, b, n_new, NSA_GROUP), _head_major(o_win, b, n_new, NSA_GROUP),
                          gates, w['g_out_sb'], w['g_out_nsa'])
    y = _finish_layer(x, mixed, mod, w)
    st = lambda a: _per_seq(a, b, n_new)
    states = (st(sbk), st(sbv), st(ck), st(cv), st(sk), st(sv), win_k_new, win_v_new)
    return y, states


def kernel(x_prompt, x_sample, cache_sb_k, cache_sb_v, cache_cmp_k, cache_cmp_v, cache_slc_k, cache_slc_v, cache_win_k, cache_win_v, page_table, c_prompt, c_sample, w_ada, b_ada, g_pre_mix, w_in, cmp_pe_k, w_cmp_k1, w_cmp_k2, cmp_pe_v, w_cmp_v1, w_cmp_v2, g_out_sb, g_out_nsa, w_o, g_post_mix, g_pre_ffn, w_gate, w_up, w_down, g_post_ffn):
    depth = w_ada.shape[0]
    bp, t, d = x_prompt.shape
    bs, n_new, _ = x_sample.shape
    assert bp == 1
    y_p = x_prompt.reshape(t, d)
    y_s = jnp.swapaxes(x_sample, 0, 1).reshape(n_new * bs, d)
    per_layer = []
    for l in range(depth):
        w = {'g_pre_mix': g_pre_mix[l], 'w_in': w_in[l], 'g_out_sb': g_out_sb[l], 'g_out_nsa': g_out_nsa[l],
             'w_o': w_o[l], 'g_post_mix': g_post_mix[l], 'g_pre_ffn': g_pre_ffn[l], 'w_gate': w_gate[l],
             'w_up': w_up[l], 'w_down': w_down[l], 'g_post_ffn': g_post_ffn[l]}
        n_c = bp + bs
        c_all = jnp.pad(jnp.concatenate([c_prompt, c_sample], axis=0), ((0, -n_c % 16), (0, 0)))
        mod = matmul(c_all, w_ada[l], tm=c_all.shape[0], tn=512, bias=b_ada[l], a_silu=True, name="ada_mod")
        mod_p = jnp.split(mod[:bp], N_ADA, axis=-1)
        mod_s = jnp.split(mod[bp:n_c], N_ADA, axis=-1)
        w1k, w2k = _cmp_weights(w_cmp_k1[l], w_cmp_k2[l])
        w1v, w2v = _cmp_weights(w_cmp_v1[l], w_cmp_v2[l])
        cmp_w = (jnp.stack([w1k, w1v]),
                 jnp.stack([_pe_bias(cmp_pe_k[l], w_cmp_k1[l]), _pe_bias(cmp_pe_v[l], w_cmp_v1[l])]),
                 jnp.stack([w2k, w2v]))
        y_p, st_p = _prompt_layer(y_p, mod_p, w, cmp_w)
        caches = (cache_sb_k, cache_sb_v, cache_cmp_k, cache_cmp_v,
                  cache_slc_k, cache_slc_v, cache_win_k, cache_win_v)
        y_s, st_s = _sample_layer(y_s, mod_s, w, cmp_w, caches, page_table, l)
        per_layer.append(st_p + st_s)
    new = [jnp.stack(s) for s in zip(*per_layer)]
    return (y_p.reshape(bp, t, d), jnp.swapaxes(y_s.reshape(n_new, bs, d), 0, 1), *new)
```
